```python
import math
import jax, jax.numpy as jnp
from jax import lax
import numpy as np

D_MODEL = 1024
BATCH = 4
SEQ = 4096
DEPTH = 4

HEAD_DIM = 64
FOURIER_WIDTH = D_MODEL // 4
FOURIER_CH = HEAD_DIM
N_FOURIER_GROUPS = FOURIER_WIDTH // FOURIER_CH
ATTN_WIDTH = D_MODEL // 2
N_Q_HEADS = ATTN_WIDTH // HEAD_DIM
N_KV_HEADS = 2
KV_WIDTH = N_KV_HEADS * HEAD_DIM
MEM_WIDTH = D_MODEL // 4
N_MEM_HEADS = MEM_WIDTH // HEAD_DIM
N_MEM = 256
MIX_WIDTH = FOURIER_WIDTH + ATTN_WIDTH + MEM_WIDTH
IN_WIDTH = FOURIER_WIDTH + ATTN_WIDTH + 2 * KV_WIDTH + MEM_WIDTH
WINDOW = 128
BLOCK = 128
ROPE_THETA = 10000.0
D_FF = -(-8 * D_MODEL // (3 * 256)) * 256
EPS = 1e-6
NEG_INF = -1e30

kernel_name = 'hybrid_fourier_window_memory_encoder'


def _rmsnorm(x, g):
    xf = x.astype(jnp.float32)
    y = xf * lax.rsqrt(jnp.mean(xf * xf, axis=-1, keepdims=True) + EPS)
    return (y * g.astype(jnp.float32)).astype(x.dtype)


def _rope(t, positions):
    half = HEAD_DIM // 2
    inv_freq = jnp.exp(-math.log(ROPE_THETA) * jnp.arange(half, dtype=jnp.float32) * (2.0 / HEAD_DIM))
    ang = positions.astype(jnp.float32)[:, :, None] * inv_freq
    cos = jnp.cos(ang)[:, :, None, :]
    sin = jnp.sin(ang)[:, :, None, :]
    tf = t.astype(jnp.float32)
    t1, t2 = tf[..., :half], tf[..., half:]
    return jnp.concatenate([t1 * cos - t2 * sin, t2 * cos + t1 * sin], axis=-1).astype(t.dtype)


def _fourier_mix(u, w_f):
    b, s, _ = u.shape
    ug = u.reshape(b, s, N_FOURIER_GROUPS, FOURIER_CH).astype(jnp.float32)
    spec = jnp.fft.fftn(ug, axes=(1, 3), norm='ortho').real
    y = jnp.einsum('bsgc,gce->bsge', spec.astype(u.dtype), w_f)
    return y.reshape(b, s, FOURIER_WIDTH)


def _window_attn(q, k, v, sink):
    b, s, _, _ = q.shape
    nb = s // BLOCK
    grp = N_Q_HEADS // N_KV_HEADS

    def bands(t):
        tp = jnp.pad(t, ((0, 0), (BLOCK, BLOCK), (0, 0), (0, 0)))
        tp = tp.reshape(b, nb + 2, BLOCK, N_KV_HEADS, HEAD_DIM)
        return jnp.concatenate([tp[:, :-2], tp[:, 1:-1], tp[:, 2:]], axis=2)

    kb, vb = bands(k), bands(v)
    qb = q.reshape(b, nb, BLOCK, N_KV_HEADS, grp, HEAD_DIM)
    scores = jnp.einsum('bnqhgd,bnjhd->bnhgqj', qb, kb).astype(jnp.float32) * (HEAD_DIM ** -0.5)
    blk = jnp.arange(nb)[:, None, None] * BLOCK
    qpos = blk + jnp.arange(BLOCK)[None, :, None]
    kpos = blk - BLOCK + jnp.arange(3 * BLOCK)[None, None, :]
    valid = (kpos >= 0) & (kpos < s) & (jnp.abs(qpos - kpos) <= WINDOW)
    scores = jnp.where(valid[None, :, None, None], scores, NEG_INF)
    sink_b = jnp.broadcast_to(sink.astype(jnp.float32).reshape(1, 1, N_KV_HEADS, grp, 1, 1),
                              scores.shape[:-1] + (1,))
    probs = jax.nn.softmax(jnp.concatenate([scores, sink_b], axis=-1), axis=-1)[..., :-1]
    out = jnp.einsum('bnhgqj,bnjhd->bnqhgd', probs.astype(v.dtype), vb)
    return out.reshape(b, s, N_Q_HEADS * HEAD_DIM)


def _memory_attn(q, km, vm):
    b, s, _, _ = q.shape
    scores = jnp.einsum('bshd,bmhd->bhsm', q, km).astype(jnp.float32) * (HEAD_DIM ** -0.5)
    probs = jax.nn.softmax(scores, axis=-1)
    out = jnp.einsum('bhsm,bmhd->bshd', probs.astype(vm.dtype), vm)
    return out.reshape(b, s, MEM_WIDTH)


def setup_inputs(seed: int = 0) -> dict:
    key = jax.random.key(seed)
    ks = jax.random.split(key, 20)
    f32 = jnp.float32

    def nrm(k, shape, scale):
        return jax.random.normal(k, shape, f32) * scale

    def gain(k, shape):
        return 1.0 + 0.05 * jax.random.normal(k, shape, f32)

    x = jax.random.normal(ks[0], (BATCH, SEQ, D_MODEL), f32)
    mem = jax.random.normal(ks[1], (BATCH, N_MEM, D_MODEL), f32)
    positions = jnp.broadcast_to(jnp.arange(SEQ, dtype=jnp.int32)[None, :], (BATCH, SEQ))
    return {
        'x': x,
        'mem': mem,
        'positions': positions,
        'g_pre_mix': gain(ks[2], (DEPTH, D_MODEL)),
        'w_in': nrm(ks[3], (DEPTH, D_MODEL, IN_WIDTH), D_MODEL ** -0.5),
        'w_fourier': nrm(ks[4], (DEPTH, N_FOURIER_GROUPS, FOURIER_CH, FOURIER_CH), FOURIER_CH ** -0.5),
        'sink': nrm(ks[5], (DEPTH, N_Q_HEADS), 0.5),
        'g_mem': gain(ks[6], (DEPTH, D_MODEL)),
        'w_mem_kv': nrm(ks[7], (DEPTH, D_MODEL, 2 * MEM_WIDTH), D_MODEL ** -0.5),
        'g_grp': gain(ks[8], (DEPTH, MIX_WIDTH)),
        'w_out': nrm(ks[9], (DEPTH, MIX_WIDTH, D_MODEL), MIX_WIDTH ** -0.5),
        'g_post_mix': gain(ks[10], (DEPTH, D_MODEL)),
        'g_pre_ffn': gain(ks[11], (DEPTH, D_MODEL)),
        'w_ffn_in': nrm(ks[12], (DEPTH, D_MODEL, 2 * D_FF), D_MODEL ** -0.5),
        'w_ffn_out': nrm(ks[13], (DEPTH, D_FF, D_MODEL), D_FF ** -0.5),
        'g_post_ffn': gain(ks[14], (DEPTH, D_MODEL)),
    }


def reference(x, mem, positions, g_pre_mix, w_in, w_fourier, sink, g_mem, w_mem_kv, g_grp,
              w_out, g_post_mix, g_pre_ffn, w_ffn_in, w_ffn_out, g_post_ffn):
    b, s, _ = x.shape
    m = mem.shape[1]
    splits = [FOURIER_WIDTH,
              FOURIER_WIDTH + ATTN_WIDTH,
              FOURIER_WIDTH + ATTN_WIDTH + KV_WIDTH,
              FOURIER_WIDTH + ATTN_WIDTH + 2 * KV_WIDTH]
    for l in range(DEPTH):
        h = _rmsnorm(x, g_pre_mix[l])
        z = h @ w_in[l]
        zf, zq, zk, zv, zm = jnp.split(z, splits, axis=-1)

        y_f = _fourier_mix(zf, w_fourier[l])

        q = _rope(zq.reshape(b, s, N_Q_HEADS, HEAD_DIM), positions)
        k = _rope(zk.reshape(b, s, N_KV_HEADS, HEAD_DIM), positions)
        v = zv.reshape(b, s, N_KV_HEADS, HEAD_DIM)
        y_a = _window_attn(q, k, v, sink[l])

        mkv = _rmsnorm(mem, g_mem[l]) @ w_mem_kv[l]
        km, vm = jnp.split(mkv, 2, axis=-1)
        km = km.reshape(b, m, N_MEM_HEADS, HEAD_DIM)
        vm = vm.reshape(b, m, N_MEM_HEADS, HEAD_DIM)
        y_m = _memory_attn(zm.reshape(b, s, N_MEM_HEADS, HEAD_DIM), km, vm)

        gg = g_grp[l]
        y = jnp.concatenate([
            _rmsnorm(y_f, gg[:FOURIER_WIDTH]),
            _rmsnorm(y_a, gg[FOURIER_WIDTH:FOURIER_WIDTH + ATTN_WIDTH]),
            _rmsnorm(y_m, gg[FOURIER_WIDTH + ATTN_WIDTH:]),
        ], axis=-1) @ w_out[l]
        x = x + _rmsnorm(y, g_post_mix[l])

        h = _rmsnorm(x, g_pre_ffn[l])
        gate, up = jnp.split(h @ w_ffn_in[l], 2, axis=-1)
        f = (jax.nn.silu(gate) * up) @ w_ffn_out[l]
        x = x + _rmsnorm(f, g_post_ffn[l])
    return x
```

```python
import functools
import math

import jax
import jax.numpy as jnp
from jax import lax
from jax.experimental import pallas as pl
from jax.experimental.pallas import tpu as pltpu

D_MODEL = 1024
DEPTH = 4
HEAD_DIM = 64
FOURIER_WIDTH = 256
FOURIER_CH = 64
N_FOURIER_GROUPS = 4
ATTN_WIDTH = 512
N_Q_HEADS = 8
N_KV_HEADS = 2
KV_WIDTH = 128
MEM_WIDTH = 256
N_MEM_HEADS = 4
IN_WIDTH = 1280
WINDOW = 128
BLOCK = 128
ROPE_THETA = 10000.0
D_FF = 2816
EPS = 1e-6
NEG_INF = -1e30

LANES = 128
VMEM_LIMIT = 56 * 1024 * 1024
FF_CHUNK = 256
N_FF_CHUNKS = D_FF // FF_CHUNK
Q_SCALE = HEAD_DIM ** -0.5

F32 = jnp.float32
BF16 = jnp.bfloat16


def _rms(y):
    return y * lax.rsqrt(jnp.mean(y * y, axis=-1, keepdims=True) + EPS)


def _lane_low_half(shape):
    lane = lax.broadcasted_iota(jnp.int32, shape, len(shape) - 1)
    return (lane % LANES) < HEAD_DIM


def _rope_table_kernel(pos_ref, cos_ref, sin_ref):
    lane = lax.broadcasted_iota(jnp.int32, pos_ref.shape, 1)
    f = (lane % (HEAD_DIM // 2)).astype(F32)
    inv_freq = jnp.exp(-math.log(ROPE_THETA) * f * (2.0 / HEAD_DIM))
    ang = pos_ref[...] * inv_freq
    cos_ref[...] = jnp.cos(ang)
    sin_ref[...] = jnp.sin(ang)


def _rope_tables(positions):
    t = positions.size
    half = HEAD_DIM // 2
    rows = t * half // LANES
    pos = jnp.repeat(positions.reshape(t).astype(F32), half).reshape(rows, LANES)
    tr = 512
    cos_c, sin_c = pl.pallas_call(
        _rope_table_kernel,
        grid=(rows // tr,),
        in_specs=[pl.BlockSpec((tr, LANES), lambda i: (i, 0))],
        out_specs=[pl.BlockSpec((tr, LANES), lambda i: (i, 0))] * 2,
        out_shape=[jax.ShapeDtypeStruct((rows, LANES), F32)] * 2,
        name="rope_tables",
    )(pos)
    cos32 = cos_c.reshape(t, half)
    sin32 = sin_c.reshape(t, half)
    cos_t = jnp.tile(cos32, (1, 4))
    sin_t = jnp.tile(jnp.concatenate([-sin32, sin32], axis=1), (1, 2))
    return cos_t, sin_t


def _dft_matrix_kernel(c_ref, s_ref, ac_ref, as_ref, bc_ref, bs_ref, *, n):
    j1 = pl.program_id(0)
    rows = ac_ref.shape[0]

    @pl.when(j1 == 0)
    def _():
        r = lax.broadcasted_iota(jnp.int32, ac_ref.shape, 0)
        k = lax.broadcasted_iota(jnp.int32, ac_ref.shape, 1)
        a = (((r * k) & (rows - 1)).astype(F32)) * (2.0 * math.pi / rows)
        b = (((r * k) & (n - 1)).astype(F32)) * (2.0 * math.pi / n)
        ac_ref[...] = jnp.cos(a)
        as_ref[...] = jnp.sin(a)
        bc_ref[...] = jnp.cos(b)
        bs_ref[...] = jnp.sin(b)

    ca = ac_ref[pl.ds(j1, 1), :]
    sa = as_ref[pl.ds(j1, 1), :]
    cb = bc_ref[...]
    sb = bs_ref[...]
    c_ref[...] = (ca * cb - sa * sb).astype(BF16)
    s_ref[...] = (-(sa * cb + ca * sb)).astype(BF16)


def _dft_matrices(n):
    rows = 64
    assert n % rows == 0 and n // rows == rows
    return pl.pallas_call(
        functools.partial(_dft_matrix_kernel, n=n),
        grid=(n // rows,),
        out_specs=[pl.BlockSpec((rows, n), lambda i: (i, 0))] * 2,
        out_shape=[jax.ShapeDtypeStruct((n, n), BF16)] * 2,
        scratch_shapes=[pltpu.VMEM((rows, n), F32)] * 4,
        compiler_params=pltpu.CompilerParams(dimension_semantics=("arbitrary",)),
        name="dft_matrices",
    )()


def _fourier_fold_kernel(w_ref, ab_ref):
    shp = (FOURIER_WIDTH, FOURIER_WIDTH)
    r = lax.broadcasted_iota(jnp.int32, shp, 0)
    c = lax.broadcasted_iota(jnp.int32, shp, 1)
    same = (r // FOURIER_CH) == (c // FOURIER_CH)
    ang = ((((r % FOURIER_CH) * (c % FOURIER_CH)) % FOURIER_CH).astype(F32)) * (2.0 * math.pi / FOURIER_CH)
    cm = jnp.where(same, jnp.cos(ang), 0.0)
    sm = jnp.where(same, jnp.sin(ang), 0.0)
    w = w_ref[...]
    a = jnp.dot(cm, w, preferred_element_type=F32, precision=lax.Precision.HIGHEST)
    b = jnp.dot(sm, w, preferred_element_type=F32, precision=lax.Precision.HIGHEST)
    ab_ref[:, :FOURIER_WIDTH] = a.astype(BF16)
    ab_ref[:, FOURIER_WIDTH:] = b.astype(BF16)


def _fourier_fold(w_fourier):
    eye = jnp.eye(N_FOURIER_GROUPS, dtype=F32)
    w_bd = jnp.einsum("lgce,gh->lgche", w_fourier, eye).reshape(DEPTH, FOURIER_WIDTH, FOURIER_WIDTH)
    return pl.pallas_call(
        _fourier_fold_kernel,
        grid=(DEPTH,),
        in_specs=[pl.BlockSpec((None, FOURIER_WIDTH, FOURIER_WIDTH), lambda l: (l, 0, 0))],
        out_specs=pl.BlockSpec((None, FOURIER_WIDTH, 2 * FOURIER_WIDTH), lambda l: (l, 0, 0)),
        out_shape=jax.ShapeDtypeStruct((DEPTH, FOURIER_WIDTH, 2 * FOURIER_WIDTH), BF16),
        name="fourier_fold",
    )(w_bd)


def _mem_kv_kernel(mem_ref, g_ref, w_ref, kc_ref, vc_ref):
    h = (_rms(mem_ref[...]) * g_ref[...]).astype(BF16)
    mkv = jnp.dot(h, w_ref[...], preferred_element_type=F32)
    low = _lane_low_half((mem_ref.shape[0], LANES))
    m = mem_ref.shape[0]
    for a in range(N_MEM_HEADS // 2):
        kp = mkv[:, a * LANES:(a + 1) * LANES]
        vp = mkv[:, MEM_WIDTH + a * LANES:MEM_WIDTH + (a + 1) * LANES]
        kc_ref[a, :m, :] = jnp.where(low, kp, 0.0).astype(BF16)
        kc_ref[a, m:, :] = jnp.where(low, 0.0, kp).astype(BF16)
        vc_ref[a, :m, :] = jnp.where(low, vp, 0.0).astype(BF16)
        vc_ref[a, m:, :] = jnp.where(low, 0.0, vp).astype(BF16)


def _mem_kv(mem, g_mem, w_mem_kv_b):
    b, m, _ = mem.shape
    npair = N_MEM_HEADS // 2
    out = jax.ShapeDtypeStruct((DEPTH, b, npair, 2 * m, LANES), BF16)
    spec = pl.BlockSpec((None, None, npair, 2 * m, LANES), lambda l, i: (l, i, 0, 0, 0))
    return pl.pallas_call(
        _mem_kv_kernel,
        grid=(DEPTH, b),
        in_specs=[
            pl.BlockSpec((None, m, D_MODEL), lambda l, i: (i, 0, 0)),
            pl.BlockSpec((None, 1, D_MODEL), lambda l, i: (l, 0, 0)),
            pl.BlockSpec((None, D_MODEL, 2 * MEM_WIDTH), lambda l, i: (l, 0, 0)),
        ],
        out_specs=[spec, spec],
        out_shape=[out, out],
        name="mem_kv",
    )(mem, g_mem.reshape(DEPTH, 1, D_MODEL), w_mem_kv_b)


def _in_proj_kernel(x_ref, g_ref, w_ref, ab_ref, cos_ref, sin_ref,
                    q_ref, k_ref, v_ref, zm_ref, p_ref, qq_ref):
    h = (_rms(x_ref[...]) * g_ref[...]).astype(BF16)
    z = jnp.dot(h, w_ref[...], preferred_element_type=F32)
    tm = z.shape[0]

    zf = z[:, :FOURIER_WIDTH].astype(BF16)
    pq = jnp.dot(zf, ab_ref[...], preferred_element_type=F32)
    p_ref[...] = pq[:, :FOURIER_WIDTH].astype(BF16)
    qq_ref[...] = pq[:, FOURIER_WIDTH:].astype(BF16)

    cos = cos_ref[...]
    sin = sin_ref[...]
    first = (lax.broadcasted_iota(jnp.int32, (tm, LANES), 1) % HEAD_DIM) < (HEAD_DIM // 2)

    def rope(t, c, s):
        rot = jnp.where(first, pltpu.roll(t, LANES - HEAD_DIM // 2, 1), pltpu.roll(t, HEAD_DIM // 2, 1))
        return t * c + rot * s

    q0 = FOURIER_WIDTH
    cq = cos * Q_SCALE
    sq = sin * Q_SCALE
    for c in range(ATTN_WIDTH // LANES):
        t = z[:, q0 + c * LANES:q0 + (c + 1) * LANES]
        q_ref[:, c * LANES:(c + 1) * LANES] = rope(t, cq, sq).astype(BF16)

    k0 = q0 + ATTN_WIDTH
    low = _lane_low_half((tm, LANES))
    kk = rope(z[:, k0:k0 + KV_WIDTH], cos, sin)
    vv = z[:, k0 + KV_WIDTH:k0 + 2 * KV_WIDTH]
    for src, dst in ((kk, k_ref), (vv, v_ref)):
        sw = pltpu.roll(src, HEAD_DIM, 1)
        dst[:, 0 * LANES:1 * LANES] = jnp.where(low, src, 0.0).astype(BF16)
        dst[:, 1 * LANES:2 * LANES] = jnp.where(low, 0.0, sw).astype(BF16)
        dst[:, 2 * LANES:3 * LANES] = jnp.where(low, sw, 0.0).astype(BF16)
        dst[:, 3 * LANES:4 * LANES] = jnp.where(low, 0.0, src).astype(BF16)

    m0 = k0 + 2 * KV_WIDTH
    zm_ref[...] = (z[:, m0:m0 + MEM_WIDTH] * Q_SCALE).astype(BF16)


def _in_proj(x, g, w_in_b, ab, cos_t, sin_t, *, tm):
    b, s, _ = x.shape
    nt = s // tm
    tok = lambda w: pl.BlockSpec((None, tm, w), lambda i, j: (i, j, 0))
    const2 = lambda r, c: pl.BlockSpec((r, c), lambda i, j: (0, 0))
    seq_major = pl.BlockSpec((tm, FOURIER_WIDTH), lambda i, j: (j, i))
    return pl.pallas_call(
        _in_proj_kernel,
        grid=(b, nt),
        in_specs=[
            tok(D_MODEL),
            const2(1, D_MODEL),
            const2(D_MODEL, IN_WIDTH),
            const2(FOURIER_WIDTH, 2 * FOURIER_WIDTH),
            pl.BlockSpec((tm, LANES), lambda i, j: (i * nt + j, 0)),
            pl.BlockSpec((tm, LANES), lambda i, j: (i * nt + j, 0)),
        ],
        out_specs=[tok(ATTN_WIDTH), tok(4 * LANES), tok(4 * LANES), tok(MEM_WIDTH), seq_major, seq_major],
        out_shape=[
            jax.ShapeDtypeStruct((b, s, ATTN_WIDTH), BF16),
            jax.ShapeDtypeStruct((b, s, 4 * LANES), BF16),
            jax.ShapeDtypeStruct((b, s, 4 * LANES), BF16),
            jax.ShapeDtypeStruct((b, s, MEM_WIDTH), BF16),
            jax.ShapeDtypeStruct((s, b * FOURIER_WIDTH), BF16),
            jax.ShapeDtypeStruct((s, b * FOURIER_WIDTH), BF16),
        ],
        compiler_params=pltpu.CompilerParams(
            dimension_semantics=("arbitrary", "arbitrary"), vmem_limit_bytes=VMEM_LIMIT),
        name="in_proj",
    )(x, g.reshape(1, D_MODEL), w_in_b, ab, cos_t, sin_t)


def _softmax_parts(sc, sink):
    m = jnp.max(sc, axis=-1, keepdims=True)
    if sink is not None:
        m = jnp.maximum(m, sink)
    pe = jnp.exp(sc - m)
    den = jnp.sum(pe, axis=-1, keepdims=True)
    if sink is not None:
        den = den + jnp.exp(sink - m)
    return pe.astype(BF16), 1.0 / den


def _attn_kernel(sink_ref, q_ref, k_ref, v_ref, zm_ref, kmc_ref, vmc_ref, ga_ref, gm_ref,
                 ya_ref, ym_ref, *, seq):
    tq = q_ref.shape[0]
    nblk = tq // BLOCK
    band = 3 * BLOCK
    t = pl.program_id(1)
    rel = (lax.broadcasted_iota(jnp.int32, (BLOCK, band), 0)
           - lax.broadcasted_iota(jnp.int32, (BLOCK, band), 1))
    low = _lane_low_half((BLOCK, LANES))
    pairs_per_kv = N_Q_HEADS // N_KV_HEADS // 2

    for jj in range(nblk):
        blk = t * nblk + jj
        s0 = pl.multiple_of(jnp.clip((blk - 1) * BLOCK, 0, seq - band), BLOCK)
        d = blk * BLOCK - s0
        valid = jnp.abs(rel + d) <= WINDOW
        rows = pl.ds(jj * BLOCK, BLOCK)
        outs = []
        for g in range(N_KV_HEADS):
            kcat = jnp.concatenate(
                [k_ref[pl.ds(s0, band), (2 * g) * LANES:(2 * g + 1) * LANES],
                 k_ref[pl.ds(s0, band), (2 * g + 1) * LANES:(2 * g + 2) * LANES]], axis=0)
            vcat = jnp.concatenate(
                [v_ref[pl.ds(s0, band), (2 * g) * LANES:(2 * g + 1) * LANES],
                 v_ref[pl.ds(s0, band), (2 * g + 1) * LANES:(2 * g + 2) * LANES]], axis=0)
            qg = jnp.concatenate(
                [q_ref[rows, (pairs_per_kv * g + p) * LANES:(pairs_per_kv * g + p + 1) * LANES]
                 for p in range(pairs_per_kv)], axis=0)
            sc = lax.dot_general(qg, kcat, (((1,), (1,)), ((), ())), preferred_element_type=F32)
            probs, recips = [], []
            for p in range(pairs_per_kv):
                row_p, row_r = [], []
                for e in range(2):
                    head = (N_Q_HEADS // N_KV_HEADS) * g + 2 * p + e
                    sb = sc[p * BLOCK:(p + 1) * BLOCK, e * band:(e + 1) * band]
                    sb = jnp.where(valid, sb, NEG_INF)
                    pe, r = _softmax_parts(sb, sink_ref[head])
                    row_p.append(pe)
                    row_r.append(r)
                probs.append(jnp.concatenate(row_p, axis=1))
                recips.append(row_r)
            pm = jnp.concatenate(probs, axis=0)
            o = jnp.dot(pm, vcat, preferred_element_type=F32)
            for p in range(pairs_per_kv):
                scale = jnp.where(low, recips[p][0], recips[p][1])
                outs.append(o[p * BLOCK:(p + 1) * BLOCK, :] * scale)
        y = jnp.concatenate(outs, axis=1)
        ya_ref[rows, :] = (_rms(y) * ga_ref[...]).astype(BF16)

    nmem = kmc_ref.shape[1] // 2
    low_t = _lane_low_half((tq, LANES))
    outs = []
    for a in range(N_MEM_HEADS // 2):
        qa = zm_ref[:, a * LANES:(a + 1) * LANES]
        sc = lax.dot_general(qa, kmc_ref[a], (((1,), (1,)), ((), ())), preferred_element_type=F32)
        p0, r0 = _softmax_parts(sc[:, :nmem], None)
        p1, r1 = _softmax_parts(sc[:, nmem:], None)
        o = jnp.dot(jnp.concatenate([p0, p1], axis=1), vmc_ref[a], preferred_element_type=F32)
        outs.append(o * jnp.where(low_t, r0, r1))
    ym = jnp.concatenate(outs, axis=1)
    ym_ref[...] = (_rms(ym) * gm_ref[...]).astype(BF16)


def _attention(sink_l, q, k4, v4, zm, kmc, vmc, g_attn, g_memgrp, *, tq):
    b, s, _ = q.shape
    nt = s // tq
    npair, m2, _ = kmc.shape[1:]
    tok = lambda w: pl.BlockSpec((None, tq, w), lambda i, j: (i, j, 0))
    per_batch = lambda w: pl.BlockSpec((None, s, w), lambda i, j: (i, 0, 0))
    memspec = pl.BlockSpec((None, npair, m2, LANES), lambda i, j: (i, 0, 0, 0))
    return pl.pallas_call(
        functools.partial(_attn_kernel, seq=s),
        grid=(b, nt),
        in_specs=[
            pl.BlockSpec(memory_space=pltpu.SMEM),
            tok(ATTN_WIDTH), per_batch(4 * LANES), per_batch(4 * LANES), tok(MEM_WIDTH),
            memspec, memspec,
            pl.BlockSpec((1, ATTN_WIDTH), lambda i, j: (0, 0)),
            pl.BlockSpec((1, MEM_WIDTH), lambda i, j: (0, 0)),
        ],
        out_specs=[tok(ATTN_WIDTH), tok(MEM_WIDTH)],
        out_shape=[jax.ShapeDtypeStruct((b, s, ATTN_WIDTH), BF16),
                   jax.ShapeDtypeStruct((b, s, MEM_WIDTH), BF16)],
        compiler_params=pltpu.CompilerParams(
            dimension_semantics=("arbitrary", "arbitrary"), vmem_limit_bytes=VMEM_LIMIT),
        name="attention",
    )(sink_l, q, k4, v4, zm, kmc, vmc, g_attn.reshape(1, ATTN_WIDTH), g_memgrp.reshape(1, MEM_WIDTH))


def _seq_dft_kernel(c_ref, s_ref, p_ref, q_ref, g_ref, y_ref, *, scale):
    y = (jnp.dot(c_ref[...], p_ref[...], preferred_element_type=F32)
         + jnp.dot(s_ref[...], q_ref[...], preferred_element_type=F32)) * scale
    nb = y.shape[1] // FOURIER_WIDTH
    for i in range(nb):
        yi = y[:, i * FOURIER_WIDTH:(i + 1) * FOURIER_WIDTH]
        y_ref[:, i * FOURIER_WIDTH:(i + 1) * FOURIER_WIDTH] = (_rms(yi) * g_ref[...]).astype(BF16)


def _seq_dft(cmat, smat, p2, q2, g_f, *, tmf):
    s, w = p2.shape
    scale = 1.0 / math.sqrt(s * FOURIER_CH)
    resident = pl.BlockSpec((s, w), lambda i: (0, 0), pipeline_mode=pl.Buffered(1))
    return pl.pallas_call(
        functools.partial(_seq_dft_kernel, scale=scale),
        grid=(s // tmf,),
        in_specs=[
            pl.BlockSpec((tmf, s), lambda i: (i, 0)),
            pl.BlockSpec((tmf, s), lambda i: (i, 0)),
            resident, resident,
            pl.BlockSpec((1, FOURIER_WIDTH), lambda i: (0, 0)),
        ],
        out_specs=pl.BlockSpec((tmf, w), lambda i: (i, 0)),
        out_shape=jax.ShapeDtypeStruct((s, w), BF16),
        compiler_params=pltpu.CompilerParams(
            dimension_semantics=("arbitrary",), vmem_limit_bytes=VMEM_LIMIT),
        name="seq_dft",
    )(cmat, smat, p2, q2, g_f.reshape(1, FOURIER_WIDTH))


def _out_ffn_kernel(yf_ref, ya_ref, ym_ref, x_ref, wo_ref, gpm_ref, gpf_ref, w1_ref, w2_ref, gpo_ref,
                    xo_ref, x1_ref, h_ref, acc_ref):
    ycat = jnp.concatenate([yf_ref[...], ya_ref[...], ym_ref[...]], axis=1)
    y = jnp.dot(ycat, wo_ref[...], preferred_element_type=F32)
    x1 = x_ref[...] + _rms(y) * gpm_ref[...]
    x1_ref[...] = x1
    h_ref[...] = (_rms(x1) * gpf_ref[...]).astype(BF16)
    acc_ref[...] = jnp.zeros_like(acc_ref)

    def chunk(c, carry):
        gu = jnp.dot(h_ref[...], w1_ref[c], preferred_element_type=F32)
        gate = gu[:, :FF_CHUNK]
        up = gu[:, FF_CHUNK:]
        f = (gate * jax.nn.sigmoid(gate) * up).astype(BF16)
        acc_ref[...] += jnp.dot(f, w2_ref[c], preferred_element_type=F32)
        return carry

    lax.fori_loop(0, N_FF_CHUNKS, chunk, 0)
    xo_ref[...] = x1_ref[...] + _rms(acc_ref[...]) * gpo_ref[...]


def _out_ffn(yf, ya, ym, x, w_out_b, g_post_mix, g_pre_ffn, w1, w2, g_post_ffn, *, tm):
    b, s, _ = x.shape
    nt = s // tm
    tok = lambda w: pl.BlockSpec((None, tm, w), lambda i, j: (i, j, 0))
    vec = pl.BlockSpec((1, D_MODEL), lambda i, j: (0, 0))
    one = pl.Buffered(1)
    return pl.pallas_call(
        _out_ffn_kernel,
        grid=(b, nt),
        in_specs=[
            pl.BlockSpec((tm, FOURIER_WIDTH), lambda i, j: (j, i)),
            tok(ATTN_WIDTH), tok(MEM_WIDTH), tok(D_MODEL),
            pl.BlockSpec((D_MODEL, D_MODEL), lambda i, j: (0, 0), pipeline_mode=one),
            vec, vec,
            pl.BlockSpec((N_FF_CHUNKS, D_MODEL, 2 * FF_CHUNK), lambda i, j: (0, 0, 0), pipeline_mode=one),
            pl.BlockSpec((N_FF_CHUNKS, FF_CHUNK, D_MODEL), lambda i, j: (0, 0, 0), pipeline_mode=one),
            vec,
        ],
        out_specs=tok(D_MODEL),
        out_shape=jax.ShapeDtypeStruct((b, s, D_MODEL), F32),
        scratch_shapes=[pltpu.VMEM((tm, D_MODEL), F32), pltpu.VMEM((tm, D_MODEL), BF16),
                        pltpu.VMEM((tm, D_MODEL), F32)],
        compiler_params=pltpu.CompilerParams(
            dimension_semantics=("arbitrary", "arbitrary"), vmem_limit_bytes=VMEM_LIMIT),
        name="out_ffn",
    )(yf, ya, ym, x, w_out_b, g_post_mix.reshape(1, D_MODEL), g_pre_ffn.reshape(1, D_MODEL),
      w1, w2, g_post_ffn.reshape(1, D_MODEL))


def kernel(x, mem, positions, g_pre_mix, w_in, w_fourier, sink, g_mem, w_mem_kv, g_grp,
           w_out, g_post_mix, g_pre_ffn, w_ffn_in, w_ffn_out, g_post_ffn):
    b, s, _ = x.shape
    assert s % 512 == 0 and s >= 3 * BLOCK and s == 64 * 64

    w_in_b = w_in.astype(BF16)
    w_out_b = w_out.astype(BF16)
    w_mem_kv_b = w_mem_kv.astype(BF16)
    gate_w = w_ffn_in[:, :, :D_FF].reshape(DEPTH, D_MODEL, N_FF_CHUNKS, FF_CHUNK)
    up_w = w_ffn_in[:, :, D_FF:].reshape(DEPTH, D_MODEL, N_FF_CHUNKS, FF_CHUNK)
    w1 = jnp.concatenate([gate_w, up_w], axis=-1).transpose(0, 2, 1, 3).astype(BF16)
    w2 = w_ffn_out.reshape(DEPTH, N_FF_CHUNKS, FF_CHUNK, D_MODEL).astype(BF16)

    cos_t, sin_t = _rope_tables(positions)
    cmat, smat = _dft_matrices(s)
    ab = _fourier_fold(w_fourier)
    kmc, vmc = _mem_kv(mem, g_mem, w_mem_kv_b)

    for l in range(DEPTH):
        gg = g_grp[l]
        q, k4, v4, zm, p2, q2 = _in_proj(x, g_pre_mix[l], w_in_b[l], ab[l], cos_t, sin_t, tm=512)
        ya, ym = _attention(sink[l], q, k4, v4, zm, kmc[l], vmc[l],
                            gg[FOURIER_WIDTH:FOURIER_WIDTH + ATTN_WIDTH],
                            gg[FOURIER_WIDTH + ATTN_WIDTH:], tq=512)
        yf = _seq_dft(cmat, smat, p2, q2, gg[:FOURIER_WIDTH], tmf=256)
        x = _out_ffn(yf, ya, ym, x, w_out_b[l], g_post_mix[l], g_pre_ffn[l], w1[l], w2[l],
                     g_post_ffn[l], tm=512)
    return x
```

```python
import functools
import math

import jax
import jax.numpy as jnp
from jax import lax
from jax.experimental import pallas as pl
from jax.experimental.pallas import tpu as pltpu

D_MODEL = 1024
DEPTH = 4
HEAD_DIM = 64
FOURIER_WIDTH = 256
FOURIER_CH = 64
N_FOURIER_GROUPS = 4
ATTN_WIDTH = 512
N_Q_HEADS = 8
N_KV_HEADS = 2
KV_WIDTH = 128
MEM_WIDTH = 256
N_MEM_HEADS = 4
IN_WIDTH = 1280
WINDOW = 128
BLOCK = 128
ROPE_THETA = 10000.0
D_FF = 2816
EPS = 1e-6
NEG_INF = -1e30

LANES = 128
VMEM_LIMIT = 56 * 1024 * 1024
FF_CHUNK = 256
N_FF_CHUNKS = D_FF // FF_CHUNK
LOG2E = math.log2(math.e)
Q_SCALE = HEAD_DIM ** -0.5 * LOG2E
DFT_TILE = 256
DFT_ROWS = DFT_TILE + 16

F32 = jnp.float32
BF16 = jnp.bfloat16


def _rms(y):
    return y * lax.rsqrt(jnp.mean(y * y, axis=-1, keepdims=True) + EPS)


def _lane_low_half(shape):
    lane = lax.broadcasted_iota(jnp.int32, shape, len(shape) - 1)
    return (lane % LANES) < HEAD_DIM


def _rope_table_kernel(pos_ref, cos_ref, sin_ref):
    lane = lax.broadcasted_iota(jnp.int32, pos_ref.shape, 1)
    f = (lane % (HEAD_DIM // 2)).astype(F32)
    inv_freq = jnp.exp(-math.log(ROPE_THETA) * f * (2.0 / HEAD_DIM))
    ang = pos_ref[...] * inv_freq
    cos_ref[...] = jnp.cos(ang)
    sin_ref[...] = jnp.sin(ang)


def _rope_tables(positions):
    t = positions.size
    half = HEAD_DIM // 2
    rows = t * half // LANES
    pos = jnp.repeat(positions.reshape(t).astype(F32), half).reshape(rows, LANES)
    tr = 512
    cos_c, sin_c = pl.pallas_call(
        _rope_table_kernel,
        grid=(rows // tr,),
        in_specs=[pl.BlockSpec((tr, LANES), lambda i: (i, 0))],
        out_specs=[pl.BlockSpec((tr, LANES), lambda i: (i, 0))] * 2,
        out_shape=[jax.ShapeDtypeStruct((rows, LANES), F32)] * 2,
        name="rope_tables",
    )(pos)
    cos32 = cos_c.reshape(t, half)
    sin32 = sin_c.reshape(t, half)
    cos_t = jnp.tile(cos32, (1, 4))
    sin_t = jnp.tile(jnp.concatenate([-sin32, sin32], axis=1), (1, 2))
    return cos_t, sin_t


def _dft_matrix_kernel(c_ref, s_ref, ac_ref, as_ref, bc_ref, bs_ref, *, n):
    t = pl.program_id(0)
    grp = ac_ref.shape[0]

    @pl.when(t == 0)
    def _():
        r = lax.broadcasted_iota(jnp.int32, ac_ref.shape, 0)
        k = lax.broadcasted_iota(jnp.int32, ac_ref.shape, 1)
        a = (((r * k) & (grp - 1)).astype(F32)) * (2.0 * math.pi / grp)
        b = (((r * k) & (n - 1)).astype(F32)) * (2.0 * math.pi / n)
        ac_ref[...] = jnp.cos(a)
        as_ref[...] = jnp.sin(a)
        bc_ref[...] = jnp.cos(b)
        bs_ref[...] = jnp.sin(b)

    for s in range(pl.cdiv(DFT_ROWS, grp)):
        nrow = min(grp, DFT_ROWS - s * grp)
        j1 = t * (DFT_TILE // grp) + s
        ca = ac_ref[pl.ds(j1, 1), :]
        sa = as_ref[pl.ds(j1, 1), :]
        cb = bc_ref[:nrow, :]
        sb = bs_ref[:nrow, :]
        c_ref[s * grp:s * grp + nrow, :] = (ca * cb - sa * sb).astype(BF16)
        s_ref[s * grp:s * grp + nrow, :] = (-(sa * cb + ca * sb)).astype(BF16)


def _dft_matrices(n):
    grp = 64
    nt = n // (2 * DFT_TILE)
    assert n == grp * grp and DFT_TILE % grp == 0 and DFT_TILE < DFT_ROWS <= DFT_TILE + grp
    return pl.pallas_call(
        functools.partial(_dft_matrix_kernel, n=n),
        grid=(nt,),
        out_specs=[pl.BlockSpec((None, DFT_ROWS, n), lambda i: (i, 0, 0))] * 2,
        out_shape=[jax.ShapeDtypeStruct((nt, DFT_ROWS, n), BF16)] * 2,
        scratch_shapes=[pltpu.VMEM((grp, n), F32)] * 4,
        compiler_params=pltpu.CompilerParams(dimension_semantics=("arbitrary",), vmem_limit_bytes=VMEM_LIMIT),
        name="dft_matrices",
    )()


def _fourier_fold_kernel(w_ref, ab_ref):
    shp = (FOURIER_WIDTH, FOURIER_WIDTH)
    r = lax.broadcasted_iota(jnp.int32, shp, 0)
    c = lax.broadcasted_iota(jnp.int32, shp, 1)
    same = (r // FOURIER_CH) == (c // FOURIER_CH)
    ang = ((((r % FOURIER_CH) * (c % FOURIER_CH)) % FOURIER_CH).astype(F32)) * (2.0 * math.pi / FOURIER_CH)
    cm = jnp.where(same, jnp.cos(ang), 0.0)
    sm = jnp.where(same, jnp.sin(ang), 0.0)
    w = w_ref[...]
    a = jnp.dot(cm, w, preferred_element_type=F32, precision=lax.Precision.HIGHEST)
    b = jnp.dot(sm, w, preferred_element_type=F32, precision=lax.Precision.HIGHEST)
    ab_ref[:, :FOURIER_WIDTH] = a.astype(BF16)
    ab_ref[:, FOURIER_WIDTH:] = b.astype(BF16)


def _fourier_fold(w_fourier):
    eye = jnp.eye(N_FOURIER_GROUPS, dtype=F32)
    w_bd = jnp.einsum("lgce,gh->lgche", w_fourier, eye).reshape(DEPTH, FOURIER_WIDTH, FOURIER_WIDTH)
    return pl.pallas_call(
        _fourier_fold_kernel,
        grid=(DEPTH,),
        in_specs=[pl.BlockSpec((None, FOURIER_WIDTH, FOURIER_WIDTH), lambda l: (l, 0, 0))],
        out_specs=pl.BlockSpec((None, FOURIER_WIDTH, 2 * FOURIER_WIDTH), lambda l: (l, 0, 0)),
        out_shape=jax.ShapeDtypeStruct((DEPTH, FOURIER_WIDTH, 2 * FOURIER_WIDTH), BF16),
        name="fourier_fold",
    )(w_bd)


def _mem_kv_kernel(mem_ref, g_ref, w_ref, kc_ref, vc_ref):
    h = (_rms(mem_ref[...]) * g_ref[...]).astype(BF16)
    mkv = jnp.dot(h, w_ref[...], preferred_element_type=F32)
    low = _lane_low_half((mem_ref.shape[0], LANES))
    m = mem_ref.shape[0]
    for a in range(N_MEM_HEADS // 2):
        kp = mkv[:, a * LANES:(a + 1) * LANES]
        vp = mkv[:, MEM_WIDTH + a * LANES:MEM_WIDTH + (a + 1) * LANES]
        kc_ref[a, :m, :] = jnp.where(low, kp, 0.0).astype(BF16)
        kc_ref[a, m:, :] = jnp.where(low, 0.0, kp).astype(BF16)
        vc_ref[a, :m, :] = jnp.where(low, vp, 0.0).astype(BF16)
        vc_ref[a, m:, :] = jnp.where(low, 0.0, vp).astype(BF16)


def _mem_kv(mem, g_mem, w_mem_kv_b):
    b, m, _ = mem.shape
    npair = N_MEM_HEADS // 2
    out = jax.ShapeDtypeStruct((DEPTH, b, npair, 2 * m, LANES), BF16)
    spec = pl.BlockSpec((None, None, npair, 2 * m, LANES), lambda l, i: (l, i, 0, 0, 0))
    return pl.pallas_call(
        _mem_kv_kernel,
        grid=(DEPTH, b),
        in_specs=[
            pl.BlockSpec((None, m, D_MODEL), lambda l, i: (i, 0, 0)),
            pl.BlockSpec((None, 1, D_MODEL), lambda l, i: (l, 0, 0)),
            pl.BlockSpec((None, D_MODEL, 2 * MEM_WIDTH), lambda l, i: (l, 0, 0)),
        ],
        out_specs=[spec, spec],
        out_shape=[out, out],
        name="mem_kv",
    )(mem, g_mem.reshape(DEPTH, 1, D_MODEL), w_mem_kv_b)


def _in_proj_kernel(x_ref, g_ref, w_ref, ab_ref, cos_ref, sin_ref,
                    q_ref, k_ref, v_ref, zm_ref, p_ref, qq_ref):
    h = (_rms(x_ref[...]) * g_ref[...]).astype(BF16)
    z = jnp.dot(h, w_ref[...], preferred_element_type=F32)
    tm = z.shape[0]

    zf = z[:, :FOURIER_WIDTH].astype(BF16)
    pq = jnp.dot(zf, ab_ref[...], preferred_element_type=F32)
    p_ref[...] = pq[:, :FOURIER_WIDTH].astype(BF16)
    qq_ref[...] = pq[:, FOURIER_WIDTH:].astype(BF16)

    cos = cos_ref[...]
    sin = sin_ref[...]
    first = (lax.broadcasted_iota(jnp.int32, (tm, LANES), 1) % HEAD_DIM) < (HEAD_DIM // 2)

    def rope(t, c, s):
        rot = jnp.where(first, pltpu.roll(t, LANES - HEAD_DIM // 2, 1), pltpu.roll(t, HEAD_DIM // 2, 1))
        return t * c + rot * s

    q0 = FOURIER_WIDTH
    cq = cos * Q_SCALE
    sq = sin * Q_SCALE
    for c in range(ATTN_WIDTH // LANES):
        t = z[:, q0 + c * LANES:q0 + (c + 1) * LANES]
        q_ref[:, c * LANES:(c + 1) * LANES] = rope(t, cq, sq).astype(BF16)

    k0 = q0 + ATTN_WIDTH
    low = _lane_low_half((tm, LANES))
    kk = rope(z[:, k0:k0 + KV_WIDTH], cos, sin)
    vv = z[:, k0 + KV_WIDTH:k0 + 2 * KV_WIDTH]
    for src, dst in ((kk, k_ref), (vv, v_ref)):
        sw = pltpu.roll(src, HEAD_DIM, 1)
        dst[:, 0 * LANES:1 * LANES] = jnp.where(low, src, 0.0).astype(BF16)
        dst[:, 1 * LANES:2 * LANES] = jnp.where(low, 0.0, sw).astype(BF16)
        dst[:, 2 * LANES:3 * LANES] = jnp.where(low, sw, 0.0).astype(BF16)
        dst[:, 3 * LANES:4 * LANES] = jnp.where(low, 0.0, src).astype(BF16)

    m0 = k0 + 2 * KV_WIDTH
    zm_ref[...] = (z[:, m0:m0 + MEM_WIDTH] * Q_SCALE).astype(BF16)


def _in_proj(x, g, w_in_b, ab, cos_t, sin_t, *, tm):
    b, s, _ = x.shape
    nt = s // tm
    tok = lambda w: pl.BlockSpec((None, tm, w), lambda i, j: (i, j, 0))
    const2 = lambda r, c: pl.BlockSpec((r, c), lambda i, j: (0, 0))
    seq_major = pl.BlockSpec((tm, FOURIER_WIDTH), lambda i, j: (j, i))
    return pl.pallas_call(
        _in_proj_kernel,
        grid=(b, nt),
        in_specs=[
            tok(D_MODEL),
            const2(1, D_MODEL),
            const2(D_MODEL, IN_WIDTH),
            const2(FOURIER_WIDTH, 2 * FOURIER_WIDTH),
            pl.BlockSpec((tm, LANES), lambda i, j: (i * nt + j, 0)),
            pl.BlockSpec((tm, LANES), lambda i, j: (i * nt + j, 0)),
        ],
        out_specs=[tok(ATTN_WIDTH), tok(4 * LANES), tok(4 * LANES), tok(MEM_WIDTH), seq_major, seq_major],
        out_shape=[
            jax.ShapeDtypeStruct((b, s, ATTN_WIDTH), BF16),
            jax.ShapeDtypeStruct((b, s, 4 * LANES), BF16),
            jax.ShapeDtypeStruct((b, s, 4 * LANES), BF16),
            jax.ShapeDtypeStruct((b, s, MEM_WIDTH), BF16),
            jax.ShapeDtypeStruct((s, b * FOURIER_WIDTH), BF16),
            jax.ShapeDtypeStruct((s, b * FOURIER_WIDTH), BF16),
        ],
        compiler_params=pltpu.CompilerParams(
            dimension_semantics=("arbitrary", "arbitrary"), vmem_limit_bytes=VMEM_LIMIT),
        name="in_proj",
    )(x, g.reshape(1, D_MODEL), w_in_b, ab, cos_t, sin_t)


def _softmax_parts(sc, sink):
    m = jnp.max(sc, axis=-1, keepdims=True)
    if sink is not None:
        m = jnp.maximum(m, sink)
    pe = jnp.exp2(sc - m)
    den = jnp.sum(pe, axis=-1, keepdims=True)
    if sink is not None:
        den = den + jnp.exp2(sink - m)
    return pe.astype(BF16), 1.0 / den


def _attn_kernel(sink_ref, q_ref, k_ref, v_ref, zm_ref, kmc_ref, vmc_ref, ga_ref, gm_ref,
                 ya_ref, ym_ref, *, seq):
    tq = q_ref.shape[0]
    nblk = tq // BLOCK
    band = 3 * BLOCK
    t = pl.program_id(1)
    rel = (lax.broadcasted_iota(jnp.int32, (BLOCK, band), 0)
           - lax.broadcasted_iota(jnp.int32, (BLOCK, band), 1))
    low = _lane_low_half((BLOCK, LANES))
    pairs_per_kv = N_Q_HEADS // N_KV_HEADS // 2

    for jj in range(nblk):
        blk = t * nblk + jj
        s0 = pl.multiple_of(jnp.clip((blk - 1) * BLOCK, 0, seq - band), BLOCK)
        d = blk * BLOCK - s0
        valid = jnp.abs(rel + d) <= WINDOW
        rows = pl.ds(jj * BLOCK, BLOCK)
        outs = []
        for g in range(N_KV_HEADS):
            kcat = jnp.concatenate(
                [k_ref[pl.ds(s0, band), (2 * g) * LANES:(2 * g + 1) * LANES],
                 k_ref[pl.ds(s0, band), (2 * g + 1) * LANES:(2 * g + 2) * LANES]], axis=0)
            vcat = jnp.concatenate(
                [v_ref[pl.ds(s0, band), (2 * g) * LANES:(2 * g + 1) * LANES],
                 v_ref[pl.ds(s0, band), (2 * g + 1) * LANES:(2 * g + 2) * LANES]], axis=0)
            qg = jnp.concatenate(
                [q_ref[rows, (pairs_per_kv * g + p) * LANES:(pairs_per_kv * g + p + 1) * LANES]
                 for p in range(pairs_per_kv)], axis=0)
            sc = lax.dot_general(qg, kcat, (((1,), (1,)), ((), ())), preferred_element_type=F32)
            probs, recips = [], []
            for p in range(pairs_per_kv):
                row_p, row_r = [], []
                for e in range(2):
                    head = (N_Q_HEADS // N_KV_HEADS) * g + 2 * p + e
                    sb = sc[p * BLOCK:(p + 1) * BLOCK, e * band:(e + 1) * band]
                    sb = jnp.where(valid, sb, NEG_INF)
                    pe, r = _softmax_parts(sb, sink_ref[head] * LOG2E)
                    row_p.append(pe)
                    row_r.append(r)
                probs.append(jnp.concatenate(row_p, axis=1))
                recips.append(row_r)
            pm = jnp.concatenate(probs, axis=0)
            o = jnp.dot(pm, vcat, preferred_element_type=F32)
            for p in range(pairs_per_kv):
                scale = jnp.where(low, recips[p][0], recips[p][1])
                outs.append(o[p * BLOCK:(p + 1) * BLOCK, :] * scale)
        y = jnp.concatenate(outs, axis=1)
        ya_ref[rows, :] = (_rms(y) * ga_ref[...]).astype(BF16)

    nmem = kmc_ref.shape[1] // 2
    low_t = _lane_low_half((tq, LANES))
    outs = []
    for a in range(N_MEM_HEADS // 2):
        qa = zm_ref[:, a * LANES:(a + 1) * LANES]
        sc = lax.dot_general(qa, kmc_ref[a], (((1,), (1,)), ((), ())), preferred_element_type=F32)
        p0, r0 = _softmax_parts(sc[:, :nmem], None)
        p1, r1 = _softmax_parts(sc[:, nmem:], None)
        o = jnp.dot(jnp.concatenate([p0, p1], axis=1), vmc_ref[a], preferred_element_type=F32)
        outs.append(o * jnp.where(low_t, r0, r1))
    ym = jnp.concatenate(outs, axis=1)
    ym_ref[...] = (_rms(ym) * gm_ref[...]).astype(BF16)


def _attention(sink_l, q, k4, v4, zm, kmc, vmc, g_attn, g_memgrp, *, tq):
    b, s, _ = q.shape
    nt = s // tq
    npair, m2, _ = kmc.shape[1:]
    tok = lambda w: pl.BlockSpec((None, tq, w), lambda i, j: (i, j, 0))
    per_batch = lambda w: pl.BlockSpec((None, s, w), lambda i, j: (i, 0, 0))
    memspec = pl.BlockSpec((None, npair, m2, LANES), lambda i, j: (i, 0, 0, 0))
    return pl.pallas_call(
        functools.partial(_attn_kernel, seq=s),
        grid=(b, nt),
        in_specs=[
            pl.BlockSpec(memory_space=pltpu.SMEM),
            tok(ATTN_WIDTH), per_batch(4 * LANES), per_batch(4 * LANES), tok(MEM_WIDTH),
            memspec, memspec,
            pl.BlockSpec((1, ATTN_WIDTH), lambda i, j: (0, 0)),
            pl.BlockSpec((1, MEM_WIDTH), lambda i, j: (0, 0)),
        ],
        out_specs=[tok(ATTN_WIDTH), tok(MEM_WIDTH)],
        out_shape=[jax.ShapeDtypeStruct((b, s, ATTN_WIDTH), BF16),
                   jax.ShapeDtypeStruct((b, s, MEM_WIDTH), BF16)],
        compiler_params=pltpu.CompilerParams(
            dimension_semantics=("arbitrary", "arbitrary"), vmem_limit_bytes=VMEM_LIMIT),
        name="attention",
    )(sink_l, q, k4, v4, zm, kmc, vmc, g_attn.reshape(1, ATTN_WIDTH), g_memgrp.reshape(1, MEM_WIDTH))


def _seq_dft_kernel(c_ref, s_ref, p_ref, q_ref, g_ref, y_ref, hi_ref, *, scale):
    i = pl.program_id(0)
    nt = hi_ref.shape[0]

    def group_norm(y):
        nb = y.shape[1] // FOURIER_WIDTH
        return jnp.concatenate(
            [(_rms(y[:, b * FOURIER_WIDTH:(b + 1) * FOURIER_WIDTH]) * g_ref[...]).astype(BF16)
             for b in range(nb)], axis=1)

    @pl.when(i < nt)
    def _():
        a1 = jnp.dot(c_ref[...], p_ref[...], preferred_element_type=F32) * scale
        a2 = jnp.dot(s_ref[...], q_ref[...], preferred_element_type=F32) * scale
        y_ref[...] = group_norm((a1 + a2)[:DFT_TILE])
        hi = group_norm(a1 - a2)
        ra = lax.broadcasted_iota(jnp.int32, (DFT_TILE, DFT_TILE), 0)
        rr = lax.broadcasted_iota(jnp.int32, (DFT_TILE, DFT_TILE), 1)
        flip = jnp.where(rr == DFT_TILE - ra, 1.0, 0.0).astype(BF16)
        mirrored = jnp.dot(flip, hi[:DFT_TILE], preferred_element_type=F32).astype(BF16)
        first = lax.broadcasted_iota(jnp.int32, mirrored.shape, 0) == 0
        hi_ref[i] = jnp.where(first, hi[DFT_TILE:DFT_TILE + 1], mirrored)

    @pl.when(i >= nt)
    def _():
        y_ref[...] = hi_ref[2 * nt - 1 - i]


def _seq_dft(cmat, smat, p2, q2, g_f):
    s, w = p2.shape
    nt = cmat.shape[0]
    scale = 1.0 / math.sqrt(s * FOURIER_CH)
    resident = pl.BlockSpec((s, w), lambda i: (0, 0), pipeline_mode=pl.Buffered(1))
    half = pl.BlockSpec((None, DFT_ROWS, s), lambda i: (jnp.minimum(i, nt - 1), 0, 0))
    return pl.pallas_call(
        functools.partial(_seq_dft_kernel, scale=scale),
        grid=(2 * nt,),
        in_specs=[half, half, resident, resident, pl.BlockSpec((1, FOURIER_WIDTH), lambda i: (0, 0))],
        out_specs=pl.BlockSpec((DFT_TILE, w), lambda i: (i, 0)),
        out_shape=jax.ShapeDtypeStruct((s, w), BF16),
        scratch_shapes=[pltpu.VMEM((nt, DFT_TILE, w), BF16)],
        compiler_params=pltpu.CompilerParams(
            dimension_semantics=("arbitrary",), vmem_limit_bytes=VMEM_LIMIT),
        name="seq_dft",
    )(cmat, smat, p2, q2, g_f.reshape(1, FOURIER_WIDTH))


def _out_ffn_kernel(yf_ref, ya_ref, ym_ref, x_ref, wo_ref, gpm_ref, gpf_ref, w1_ref, w2_ref, gpo_ref,
                    xo_ref, x1_ref, h_ref, acc_ref):
    ycat = jnp.concatenate([yf_ref[...], ya_ref[...], ym_ref[...]], axis=1)
    y = jnp.dot(ycat, wo_ref[...], preferred_element_type=F32)
    x1 = x_ref[...] + _rms(y) * gpm_ref[...]
    x1_ref[...] = x1
    h_ref[...] = (_rms(x1) * gpf_ref[...]).astype(BF16)
    for c in range(N_FF_CHUNKS):
        cols = slice(c * FF_CHUNK, (c + 1) * FF_CHUNK)
        gate = jnp.dot(h_ref[...], w1_ref[:, cols], preferred_element_type=F32)
        up = jnp.dot(h_ref[...], w1_ref[:, D_FF + c * FF_CHUNK:D_FF + (c + 1) * FF_CHUNK],
                     preferred_element_type=F32)
        f = (gate * jax.nn.sigmoid(gate) * up).astype(BF16)
        part = jnp.dot(f, w2_ref[cols, :], preferred_element_type=F32)
        if c == 0:
            acc_ref[...] = part
        else:
            acc_ref[...] += part
    xo_ref[...] = x1_ref[...] + _rms(acc_ref[...]) * gpo_ref[...]


def _out_ffn(yf, ya, ym, x, w_out_b, g_post_mix, g_pre_ffn, w1, w2, g_post_ffn, *, tm):
    b, s, _ = x.shape
    nt = s // tm
    tok = lambda w: pl.BlockSpec((None, tm, w), lambda i, j: (i, j, 0))
    vec = pl.BlockSpec((1, D_MODEL), lambda i, j: (0, 0))
    one = pl.Buffered(1)
    return pl.pallas_call(
        _out_ffn_kernel,
        grid=(b, nt),
        in_specs=[
            pl.BlockSpec((tm, FOURIER_WIDTH), lambda i, j: (j, i)),
            tok(ATTN_WIDTH), tok(MEM_WIDTH), tok(D_MODEL),
            pl.BlockSpec((D_MODEL, D_MODEL), lambda i, j: (0, 0), pipeline_mode=one),
            vec, vec,
            pl.BlockSpec((D_MODEL, 2 * D_FF), lambda i, j: (0, 0), pipeline_mode=one),
            pl.BlockSpec((D_FF, D_MODEL), lambda i, j: (0, 0), pipeline_mode=one),
            vec,
        ],
        out_specs=tok(D_MODEL),
        out_shape=jax.ShapeDtypeStruct((b, s, D_MODEL), F32),
        scratch_shapes=[pltpu.VMEM((tm, D_MODEL), F32), pltpu.VMEM((tm, D_MODEL), BF16),
                        pltpu.VMEM((tm, D_MODEL), F32)],
        compiler_params=pltpu.CompilerParams(
            dimension_semantics=("arbitrary", "arbitrary"), vmem_limit_bytes=VMEM_LIMIT),
        name="out_ffn",
    )(yf, ya, ym, x, w_out_b, g_post_mix.reshape(1, D_MODEL), g_pre_ffn.reshape(1, D_MODEL),
      w1, w2, g_post_ffn.reshape(1, D_MODEL))


def kernel(x, mem, positions, g_pre_mix, w_in, w_fourier, sink, g_mem, w_mem_kv, g_grp,
           w_out, g_post_mix, g_pre_ffn, w_ffn_in, w_ffn_out, g_post_ffn):
    b, s, _ = x.shape
    assert s % 512 == 0 and s >= 3 * BLOCK and s == 64 * 64

    w_in_b = w_in.astype(BF16)
    w_out_b = w_out.astype(BF16)
    w_mem_kv_b = w_mem_kv.astype(BF16)
    w1 = w_ffn_in.astype(BF16)
    w2 = w_ffn_out.astype(BF16)

    cos_t, sin_t = _rope_tables(positions)
    cmat, smat = _dft_matrices(s)
    ab = _fourier_fold(w_fourier)
    kmc, vmc = _mem_kv(mem, g_mem, w_mem_kv_b)

    for l in range(DEPTH):
        gg = g_grp[l]
        q, k4, v4, zm, p2, q2 = _in_proj(x, g_pre_mix[l], w_in_b[l], ab[l], cos_t, sin_t, tm=512)
        ya, ym = _attention(sink[l], q, k4, v4, zm, kmc[l], vmc[l],
                            gg[FOURIER_WIDTH:FOURIER_WIDTH + ATTN_WIDTH],
                            gg[FOURIER_WIDTH + ATTN_WIDTH:], tq=512)
        yf = _seq_dft(cmat, smat, p2, q2, gg[:FOURIER_WIDTH])
        x = _out_ffn(yf, ya, ym, x, w_out_b[l], g_post_mix[l], g_pre_ffn[l], w1[l], w2[l],
                     g_post_ffn[l], tm=512)
    return x
```

```python
import functools
import math

import jax
import jax.numpy as jnp
from jax import lax
from jax.experimental import pallas as pl
from jax.experimental.pallas import tpu as pltpu

D_MODEL = 1024
DEPTH = 4
HEAD_DIM = 64
FOURIER_WIDTH = 256
FOURIER_CH = 64
N_FOURIER_GROUPS = 4
ATTN_WIDTH = 512
N_Q_HEADS = 8
N_KV_HEADS = 2
KV_WIDTH = 128
MEM_WIDTH = 256
N_MEM_HEADS = 4
IN_WIDTH = 1280
WINDOW = 128
BLOCK = 128
ROPE_THETA = 10000.0
D_FF = 2816
EPS = 1e-6
NEG_INF = -1e30

LANES = 128
VMEM_LIMIT = 56 * 1024 * 1024
FF_CHUNK = 256
N_FF_CHUNKS = D_FF // FF_CHUNK
LOG2E = math.log2(math.e)
Q_SCALE = HEAD_DIM ** -0.5 * LOG2E
DFT_TILE = 256
DFT_ROWS = DFT_TILE + 16

F32 = jnp.float32
BF16 = jnp.bfloat16


def _rms(y):
    return y * lax.rsqrt(jnp.mean(y * y, axis=-1, keepdims=True) + EPS)


def _lane_low_half(shape):
    lane = lax.broadcasted_iota(jnp.int32, shape, len(shape) - 1)
    return (lane % LANES) < HEAD_DIM


def _rope_table_kernel(pos_ref, cos_ref, sin_ref):
    lane = lax.broadcasted_iota(jnp.int32, pos_ref.shape, 1)
    f = (lane % (HEAD_DIM // 2)).astype(F32)
    inv_freq = jnp.exp(-math.log(ROPE_THETA) * f * (2.0 / HEAD_DIM))
    ang = pos_ref[...] * inv_freq
    cos_ref[...] = jnp.cos(ang)
    sin_ref[...] = jnp.sin(ang)


def _rope_tables(positions):
    t = positions.size
    half = HEAD_DIM // 2
    rows = t * half // LANES
    pos = jnp.repeat(positions.reshape(t).astype(F32), half).reshape(rows, LANES)
    tr = 512
    cos_c, sin_c = pl.pallas_call(
        _rope_table_kernel,
        grid=(rows // tr,),
        in_specs=[pl.BlockSpec((tr, LANES), lambda i: (i, 0))],
        out_specs=[pl.BlockSpec((tr, LANES), lambda i: (i, 0))] * 2,
        out_shape=[jax.ShapeDtypeStruct((rows, LANES), F32)] * 2,
        name="rope_tables",
    )(pos)
    cos32 = cos_c.reshape(t, half)
    sin32 = sin_c.reshape(t, half)
    cos_t = jnp.tile(cos32, (1, 4))
    sin_t = jnp.tile(jnp.concatenate([-sin32, sin32], axis=1), (1, 2))
    return cos_t, sin_t


def _dft_matrix_kernel(c_ref, s_ref, ac_ref, as_ref, bc_ref, bs_ref, *, n):
    t = pl.program_id(0)
    grp = ac_ref.shape[0]

    @pl.when(t == 0)
    def _():
        r = lax.broadcasted_iota(jnp.int32, ac_ref.shape, 0)
        k = lax.broadcasted_iota(jnp.int32, ac_ref.shape, 1)
        a = (((r * k) & (grp - 1)).astype(F32)) * (2.0 * math.pi / grp)
        b = (((r * k) & (n - 1)).astype(F32)) * (2.0 * math.pi / n)
        ac_ref[...] = jnp.cos(a)
        as_ref[...] = jnp.sin(a)
        bc_ref[...] = jnp.cos(b)
        bs_ref[...] = jnp.sin(b)

    for s in range(pl.cdiv(DFT_ROWS, grp)):
        nrow = min(grp, DFT_ROWS - s * grp)
        j1 = t * (DFT_TILE // grp) + s
        ca = ac_ref[pl.ds(j1, 1), :]
        sa = as_ref[pl.ds(j1, 1), :]
        cb = bc_ref[:nrow, :]
        sb = bs_ref[:nrow, :]
        c_ref[s * grp:s * grp + nrow, :] = (ca * cb - sa * sb).astype(BF16)
        s_ref[s * grp:s * grp + nrow, :] = (-(sa * cb + ca * sb)).astype(BF16)


def _dft_matrices(n):
    grp = 64
    nt = n // (2 * DFT_TILE)
    assert n == grp * grp and DFT_TILE % grp == 0 and DFT_TILE < DFT_ROWS <= DFT_TILE + grp
    return pl.pallas_call(
        functools.partial(_dft_matrix_kernel, n=n),
        grid=(nt,),
        out_specs=[pl.BlockSpec((None, DFT_ROWS, n), lambda i: (i, 0, 0))] * 2,
        out_shape=[jax.ShapeDtypeStruct((nt, DFT_ROWS, n), BF16)] * 2,
        scratch_shapes=[pltpu.VMEM((grp, n), F32)] * 4,
        compiler_params=pltpu.CompilerParams(dimension_semantics=("arbitrary",), vmem_limit_bytes=VMEM_LIMIT),
        name="dft_matrices",
    )()


def _fourier_fold_kernel(w_ref, ab_ref):
    shp = (FOURIER_WIDTH, FOURIER_WIDTH)
    r = lax.broadcasted_iota(jnp.int32, shp, 0)
    c = lax.broadcasted_iota(jnp.int32, shp, 1)
    same = (r // FOURIER_CH) == (c // FOURIER_CH)
    ang = ((((r % FOURIER_CH) * (c % FOURIER_CH)) % FOURIER_CH).astype(F32)) * (2.0 * math.pi / FOURIER_CH)
    cm = jnp.where(same, jnp.cos(ang), 0.0)
    sm = jnp.where(same, jnp.sin(ang), 0.0)
    w = w_ref[...]
    a = jnp.dot(cm, w, preferred_element_type=F32, precision=lax.Precision.HIGHEST)
    b = jnp.dot(sm, w, preferred_element_type=F32, precision=lax.Precision.HIGHEST)
    ab_ref[:, :FOURIER_WIDTH] = a.astype(BF16)
    ab_ref[:, FOURIER_WIDTH:] = b.astype(BF16)


def _fourier_fold(w_fourier):
    eye = jnp.eye(N_FOURIER_GROUPS, dtype=F32)
    w_bd = jnp.einsum("lgce,gh->lgche", w_fourier, eye).reshape(DEPTH, FOURIER_WIDTH, FOURIER_WIDTH)
    return pl.pallas_call(
        _fourier_fold_kernel,
        grid=(DEPTH,),
        in_specs=[pl.BlockSpec((None, FOURIER_WIDTH, FOURIER_WIDTH), lambda l: (l, 0, 0))],
        out_specs=pl.BlockSpec((None, FOURIER_WIDTH, 2 * FOURIER_WIDTH), lambda l: (l, 0, 0)),
        out_shape=jax.ShapeDtypeStruct((DEPTH, FOURIER_WIDTH, 2 * FOURIER_WIDTH), BF16),
        name="fourier_fold",
    )(w_bd)


def _mem_kv_kernel(mem_ref, g_ref, w_ref, kc_ref, vc_ref):
    h = (_rms(mem_ref[...]) * g_ref[...]).astype(BF16)
    mkv = jnp.dot(h, w_ref[...], preferred_element_type=F32)
    low = _lane_low_half((mem_ref.shape[0], LANES))
    m = mem_ref.shape[0]
    for a in range(N_MEM_HEADS // 2):
        kp = mkv[:, a * LANES:(a + 1) * LANES]
        vp = mkv[:, MEM_WIDTH + a * LANES:MEM_WIDTH + (a + 1) * LANES]
        kc_ref[a, :m, :] = jnp.where(low, kp, 0.0).astype(BF16)
        kc_ref[a, m:, :] = jnp.where(low, 0.0, kp).astype(BF16)
        vc_ref[a, :m, :] = jnp.where(low, vp, 0.0).astype(BF16)
        vc_ref[a, m:, :] = jnp.where(low, 0.0, vp).astype(BF16)


def _mem_kv(mem, g_mem, w_mem_kv_b):
    b, m, _ = mem.shape
    npair = N_MEM_HEADS // 2
    out = jax.ShapeDtypeStruct((DEPTH, b, npair, 2 * m, LANES), BF16)
    spec = pl.BlockSpec((None, None, npair, 2 * m, LANES), lambda l, i: (l, i, 0, 0, 0))
    return pl.pallas_call(
        _mem_kv_kernel,
        grid=(DEPTH, b),
        in_specs=[
            pl.BlockSpec((None, m, D_MODEL), lambda l, i: (i, 0, 0)),
            pl.BlockSpec((None, 1, D_MODEL), lambda l, i: (l, 0, 0)),
            pl.BlockSpec((None, D_MODEL, 2 * MEM_WIDTH), lambda l, i: (l, 0, 0)),
        ],
        out_specs=[spec, spec],
        out_shape=[out, out],
        name="mem_kv",
    )(mem, g_mem.reshape(DEPTH, 1, D_MODEL), w_mem_kv_b)


def _in_proj_body(xv, g_ref, w_ref, ab_ref, cos_ref, sin_ref, q_ref, k_ref, v_ref, zm_ref, p_ref, qq_ref):
    h = (_rms(xv) * g_ref[...]).astype(BF16)
    z = jnp.dot(h, w_ref[...], preferred_element_type=F32)
    tm = z.shape[0]

    zf = z[:, :FOURIER_WIDTH].astype(BF16)
    pq = jnp.dot(zf, ab_ref[...], preferred_element_type=F32)
    p_ref[...] = pq[:, :FOURIER_WIDTH].astype(BF16)
    qq_ref[...] = pq[:, FOURIER_WIDTH:].astype(BF16)

    cos = cos_ref[...]
    sin = sin_ref[...]
    first = (lax.broadcasted_iota(jnp.int32, (tm, LANES), 1) % HEAD_DIM) < (HEAD_DIM // 2)

    def rope(t, c, s):
        rot = jnp.where(first, pltpu.roll(t, LANES - HEAD_DIM // 2, 1), pltpu.roll(t, HEAD_DIM // 2, 1))
        return t * c + rot * s

    q0 = FOURIER_WIDTH
    cq = cos * Q_SCALE
    sq = sin * Q_SCALE
    for c in range(ATTN_WIDTH // LANES):
        t = z[:, q0 + c * LANES:q0 + (c + 1) * LANES]
        q_ref[:, c * LANES:(c + 1) * LANES] = rope(t, cq, sq).astype(BF16)

    k0 = q0 + ATTN_WIDTH
    low = _lane_low_half((tm, LANES))
    kk = rope(z[:, k0:k0 + KV_WIDTH], cos, sin)
    vv = z[:, k0 + KV_WIDTH:k0 + 2 * KV_WIDTH]
    for src, dst in ((kk, k_ref), (vv, v_ref)):
        sw = pltpu.roll(src, HEAD_DIM, 1)
        dst[:, 0 * LANES:1 * LANES] = jnp.where(low, src, 0.0).astype(BF16)
        dst[:, 1 * LANES:2 * LANES] = jnp.where(low, 0.0, sw).astype(BF16)
        dst[:, 2 * LANES:3 * LANES] = jnp.where(low, sw, 0.0).astype(BF16)
        dst[:, 3 * LANES:4 * LANES] = jnp.where(low, 0.0, src).astype(BF16)

    m0 = k0 + 2 * KV_WIDTH
    zm_ref[...] = (z[:, m0:m0 + MEM_WIDTH] * Q_SCALE).astype(BF16)


def _in_proj_kernel(x_ref, *refs):
    _in_proj_body(x_ref[...], *refs)


def _in_proj_specs(l, b, s, tm):
    nt = s // tm
    tok = lambda w: pl.BlockSpec((None, tm, w), lambda i, j: (i, j, 0))
    seq_major = pl.BlockSpec((tm, FOURIER_WIDTH), lambda i, j: (j, i))
    rope_spec = pl.BlockSpec((tm, LANES), lambda i, j: (i * nt + j, 0))
    in_specs = [
        pl.BlockSpec((None, 1, D_MODEL), lambda i, j: (l, 0, 0)),
        pl.BlockSpec((None, D_MODEL, IN_WIDTH), lambda i, j: (l, 0, 0), pipeline_mode=pl.Buffered(1)),
        pl.BlockSpec((None, FOURIER_WIDTH, 2 * FOURIER_WIDTH), lambda i, j: (l, 0, 0),
                     pipeline_mode=pl.Buffered(1)),
        rope_spec, rope_spec,
    ]
    out_specs = [tok(ATTN_WIDTH), tok(4 * LANES), tok(4 * LANES), tok(MEM_WIDTH), seq_major, seq_major]
    out_shape = [
        jax.ShapeDtypeStruct((b, s, ATTN_WIDTH), BF16),
        jax.ShapeDtypeStruct((b, s, 4 * LANES), BF16),
        jax.ShapeDtypeStruct((b, s, 4 * LANES), BF16),
        jax.ShapeDtypeStruct((b, s, MEM_WIDTH), BF16),
        jax.ShapeDtypeStruct((s, b * FOURIER_WIDTH), BF16),
        jax.ShapeDtypeStruct((s, b * FOURIER_WIDTH), BF16),
    ]
    return in_specs, out_specs, out_shape


def _in_proj(x, g_pre_mix, w_in_b, ab, cos_t, sin_t, *, l, tm):
    b, s, _ = x.shape
    in_specs, out_specs, out_shape = _in_proj_specs(l, b, s, tm)
    return pl.pallas_call(
        _in_proj_kernel,
        grid=(b, s // tm),
        in_specs=[pl.BlockSpec((None, tm, D_MODEL), lambda i, j: (i, j, 0))] + in_specs,
        out_specs=out_specs,
        out_shape=out_shape,
        compiler_params=pltpu.CompilerParams(
            dimension_semantics=("arbitrary", "arbitrary"), vmem_limit_bytes=VMEM_LIMIT),
        name="in_proj",
    )(x, g_pre_mix, w_in_b, ab, cos_t, sin_t)


def _softmax_parts(sc, sink):
    m = jnp.max(sc, axis=-1, keepdims=True)
    if sink is not None:
        m = jnp.maximum(m, sink)
    pe = jnp.exp2(sc - m)
    den = jnp.sum(pe, axis=-1, keepdims=True)
    if sink is not None:
        den = den + jnp.exp2(sink - m)
    return pe.astype(BF16), 1.0 / den


def _attn_kernel(sink_ref, q_ref, k_ref, v_ref, zm_ref, kmc_ref, vmc_ref, gg_ref,
                 ya_ref, ym_ref, *, seq, layer):
    ga = gg_ref[:, FOURIER_WIDTH:FOURIER_WIDTH + ATTN_WIDTH]
    gm = gg_ref[:, FOURIER_WIDTH + ATTN_WIDTH:]
    tq = q_ref.shape[0]
    nblk = tq // BLOCK
    band = 3 * BLOCK
    t = pl.program_id(1)
    rel = (lax.broadcasted_iota(jnp.int32, (BLOCK, band), 0)
           - lax.broadcasted_iota(jnp.int32, (BLOCK, band), 1))
    low = _lane_low_half((BLOCK, LANES))
    pairs_per_kv = N_Q_HEADS // N_KV_HEADS // 2

    for jj in range(nblk):
        blk = t * nblk + jj
        s0 = pl.multiple_of(jnp.clip((blk - 1) * BLOCK, 0, seq - band), BLOCK)
        d = blk * BLOCK - s0
        valid = jnp.abs(rel + d) <= WINDOW
        rows = pl.ds(jj * BLOCK, BLOCK)
        outs = []
        for g in range(N_KV_HEADS):
            kcat = jnp.concatenate(
                [k_ref[pl.ds(s0, band), (2 * g) * LANES:(2 * g + 1) * LANES],
                 k_ref[pl.ds(s0, band), (2 * g + 1) * LANES:(2 * g + 2) * LANES]], axis=0)
            vcat = jnp.concatenate(
                [v_ref[pl.ds(s0, band), (2 * g) * LANES:(2 * g + 1) * LANES],
                 v_ref[pl.ds(s0, band), (2 * g + 1) * LANES:(2 * g + 2) * LANES]], axis=0)
            qg = jnp.concatenate(
                [q_ref[rows, (pairs_per_kv * g + p) * LANES:(pairs_per_kv * g + p + 1) * LANES]
                 for p in range(pairs_per_kv)], axis=0)
            sc = lax.dot_general(qg, kcat, (((1,), (1,)), ((), ())), preferred_element_type=F32)
            probs, recips = [], []
            for p in range(pairs_per_kv):
                row_p, row_r = [], []
                for e in range(2):
                    head = (N_Q_HEADS // N_KV_HEADS) * g + 2 * p + e
                    sb = sc[p * BLOCK:(p + 1) * BLOCK, e * band:(e + 1) * band]
                    sb = jnp.where(valid, sb, NEG_INF)
                    pe, r = _softmax_parts(sb, sink_ref[layer, head] * LOG2E)
                    row_p.append(pe)
                    row_r.append(r)
                probs.append(jnp.concatenate(row_p, axis=1))
                recips.append(row_r)
            pm = jnp.concatenate(probs, axis=0)
            o = jnp.dot(pm, vcat, preferred_element_type=F32)
            for p in range(pairs_per_kv):
                scale = jnp.where(low, recips[p][0], recips[p][1])
                outs.append(o[p * BLOCK:(p + 1) * BLOCK, :] * scale)
        y = jnp.concatenate(outs, axis=1)
        ya_ref[rows, :] = (_rms(y) * ga).astype(BF16)

    nmem = kmc_ref.shape[1] // 2
    low_t = _lane_low_half((tq, LANES))
    outs = []
    for a in range(N_MEM_HEADS // 2):
        qa = zm_ref[:, a * LANES:(a + 1) * LANES]
        sc = lax.dot_general(qa, kmc_ref[a], (((1,), (1,)), ((), ())), preferred_element_type=F32)
        p0, r0 = _softmax_parts(sc[:, :nmem], None)
        p1, r1 = _softmax_parts(sc[:, nmem:], None)
        o = jnp.dot(jnp.concatenate([p0, p1], axis=1), vmc_ref[a], preferred_element_type=F32)
        outs.append(o * jnp.where(low_t, r0, r1))
    ym = jnp.concatenate(outs, axis=1)
    ym_ref[...] = (_rms(ym) * gm).astype(BF16)


def _attention(sink, q, k4, v4, zm, kmc, vmc, g_grp, *, l, tq):
    b, s, _ = q.shape
    nt = s // tq
    npair, m2, _ = kmc.shape[2:]
    tok = lambda w: pl.BlockSpec((None, tq, w), lambda i, j: (i, j, 0))
    per_batch = lambda w: pl.BlockSpec((None, s, w), lambda i, j: (i, 0, 0))
    memspec = pl.BlockSpec((None, None, npair, m2, LANES), lambda i, j: (l, i, 0, 0, 0))
    return pl.pallas_call(
        functools.partial(_attn_kernel, seq=s, layer=l),
        grid=(b, nt),
        in_specs=[
            pl.BlockSpec(memory_space=pltpu.SMEM),
            tok(ATTN_WIDTH), per_batch(4 * LANES), per_batch(4 * LANES), tok(MEM_WIDTH),
            memspec, memspec,
            pl.BlockSpec((None, 1, D_MODEL), lambda i, j: (l, 0, 0)),
        ],
        out_specs=[tok(ATTN_WIDTH), tok(MEM_WIDTH)],
        out_shape=[jax.ShapeDtypeStruct((b, s, ATTN_WIDTH), BF16),
                   jax.ShapeDtypeStruct((b, s, MEM_WIDTH), BF16)],
        compiler_params=pltpu.CompilerParams(
            dimension_semantics=("arbitrary", "arbitrary"), vmem_limit_bytes=VMEM_LIMIT),
        name="attention",
    )(sink, q, k4, v4, zm, kmc, vmc, g_grp)


def _seq_dft_kernel(c_ref, s_ref, p_ref, q_ref, g_ref, y_ref, hi_ref, *, scale):
    i = pl.program_id(0)
    nt = hi_ref.shape[0]

    def group_norm(y):
        nb = y.shape[1] // FOURIER_WIDTH
        return jnp.concatenate(
            [(_rms(y[:, b * FOURIER_WIDTH:(b + 1) * FOURIER_WIDTH]) * g_ref[:, :FOURIER_WIDTH]).astype(BF16)
             for b in range(nb)], axis=1)

    @pl.when(i < nt)
    def _():
        a1 = jnp.dot(c_ref[...], p_ref[...], preferred_element_type=F32) * scale
        a2 = jnp.dot(s_ref[...], q_ref[...], preferred_element_type=F32) * scale
        y_ref[...] = group_norm((a1 + a2)[:DFT_TILE])
        hi = group_norm(a1 - a2)
        ra = lax.broadcasted_iota(jnp.int32, (DFT_TILE, DFT_TILE), 0)
        rr = lax.broadcasted_iota(jnp.int32, (DFT_TILE, DFT_TILE), 1)
        flip = jnp.where(rr == DFT_TILE - ra, 1.0, 0.0).astype(BF16)
        mirrored = jnp.dot(flip, hi[:DFT_TILE], preferred_element_type=F32).astype(BF16)
        first = lax.broadcasted_iota(jnp.int32, mirrored.shape, 0) == 0
        hi_ref[i] = jnp.where(first, hi[DFT_TILE:DFT_TILE + 1], mirrored)

    @pl.when(i >= nt)
    def _():
        y_ref[...] = hi_ref[2 * nt - 1 - i]


def _seq_dft(cmat, smat, p2, q2, g_grp, *, l):
    s, w = p2.shape
    nt = cmat.shape[0]
    scale = 1.0 / math.sqrt(s * FOURIER_CH)
    resident = pl.BlockSpec((s, w), lambda i: (0, 0), pipeline_mode=pl.Buffered(1))
    half = pl.BlockSpec((None, DFT_ROWS, s), lambda i: (jnp.minimum(i, nt - 1), 0, 0))
    return pl.pallas_call(
        functools.partial(_seq_dft_kernel, scale=scale),
        grid=(2 * nt,),
        in_specs=[half, half, resident, resident, pl.BlockSpec((None, 1, D_MODEL), lambda i: (l, 0, 0))],
        out_specs=pl.BlockSpec((DFT_TILE, w), lambda i: (i, 0)),
        out_shape=jax.ShapeDtypeStruct((s, w), BF16),
        scratch_shapes=[pltpu.VMEM((nt, DFT_TILE, w), BF16)],
        compiler_params=pltpu.CompilerParams(
            dimension_semantics=("arbitrary",), vmem_limit_bytes=VMEM_LIMIT),
        name="seq_dft",
    )(cmat, smat, p2, q2, g_grp)


def _out_ffn_kernel(yf_ref, ya_ref, ym_ref, x_ref, wo_ref, gpm_ref, gpf_ref, w1_ref, w2_ref, gpo_ref,
                    *rest, fuse_next):
    if fuse_next:
        next_in, xo_ref, next_out = rest[:5], rest[5], rest[6:12]
    else:
        xo_ref = rest[0]
    x1_ref, h_ref, acc_ref = rest[-3:]
    ycat = jnp.concatenate([yf_ref[...], ya_ref[...], ym_ref[...]], axis=1)
    y = jnp.dot(ycat, wo_ref[...], preferred_element_type=F32)
    x1 = x_ref[...] + _rms(y) * gpm_ref[...]
    x1_ref[...] = x1
    h_ref[...] = (_rms(x1) * gpf_ref[...]).astype(BF16)
    for c in range(N_FF_CHUNKS):
        cols = slice(c * FF_CHUNK, (c + 1) * FF_CHUNK)
        gate = jnp.dot(h_ref[...], w1_ref[:, cols], preferred_element_type=F32)
        up = jnp.dot(h_ref[...], w1_ref[:, D_FF + c * FF_CHUNK:D_FF + (c + 1) * FF_CHUNK],
                     preferred_element_type=F32)
        f = (gate * jax.nn.sigmoid(gate) * up).astype(BF16)
        part = jnp.dot(f, w2_ref[cols, :], preferred_element_type=F32)
        if c == 0:
            acc_ref[...] = part
        else:
            acc_ref[...] += part
    x2 = x1_ref[...] + _rms(acc_ref[...]) * gpo_ref[...]
    xo_ref[...] = x2
    if fuse_next:
        _in_proj_body(x2, *next_in, *next_out)


def _out_ffn(yf, ya, ym, x, w_out_b, g_post_mix, g_pre_ffn, w1, w2, g_post_ffn, next_args, *, l, tm):
    b, s, _ = x.shape
    nt = s // tm
    tok = lambda w: pl.BlockSpec((None, tm, w), lambda i, j: (i, j, 0))
    vec = pl.BlockSpec((None, 1, D_MODEL), lambda i, j: (l, 0, 0))
    one = pl.Buffered(1)
    in_specs = [
        pl.BlockSpec((tm, FOURIER_WIDTH), lambda i, j: (j, i)),
        tok(ATTN_WIDTH), tok(MEM_WIDTH), tok(D_MODEL),
        pl.BlockSpec((None, D_MODEL, D_MODEL), lambda i, j: (l, 0, 0), pipeline_mode=one),
        vec, vec,
        pl.BlockSpec((None, D_MODEL, 2 * D_FF), lambda i, j: (l, 0, 0), pipeline_mode=one),
        pl.BlockSpec((None, D_FF, D_MODEL), lambda i, j: (l, 0, 0), pipeline_mode=one),
        vec,
    ]
    out_specs = [tok(D_MODEL)]
    out_shape = [jax.ShapeDtypeStruct((b, s, D_MODEL), F32)]
    fuse_next = next_args is not None
    if fuse_next:
        nin, nout, nshape = _in_proj_specs(l + 1, b, s, tm)
        in_specs += nin
        out_specs += nout
        out_shape += nshape
    return pl.pallas_call(
        functools.partial(_out_ffn_kernel, fuse_next=fuse_next),
        grid=(b, nt),
        in_specs=in_specs,
        out_specs=out_specs,
        out_shape=out_shape,
        scratch_shapes=[pltpu.VMEM((tm, D_MODEL), F32), pltpu.VMEM((tm, D_MODEL), BF16),
                        pltpu.VMEM((tm, D_MODEL), F32)],
        compiler_params=pltpu.CompilerParams(
            dimension_semantics=("arbitrary", "arbitrary"), vmem_limit_bytes=VMEM_LIMIT),
        name="out_ffn",
    )(yf, ya, ym, x, w_out_b, g_post_mix, g_pre_ffn, w1, w2, g_post_ffn, *(next_args or ()))


def kernel(x, mem, positions, g_pre_mix, w_in, w_fourier, sink, g_mem, w_mem_kv, g_grp,
           w_out, g_post_mix, g_pre_ffn, w_ffn_in, w_ffn_out, g_post_ffn):
    b, s, _ = x.shape
    assert s % 512 == 0 and s >= 3 * BLOCK and s == 64 * 64

    w_in_b = w_in.astype(BF16)
    w_out_b = w_out.astype(BF16)
    w_mem_kv_b = w_mem_kv.astype(BF16)
    w1 = w_ffn_in.astype(BF16)
    w2 = w_ffn_out.astype(BF16)

    cos_t, sin_t = _rope_tables(positions)
    cmat, smat = _dft_matrices(s)
    ab = _fourier_fold(w_fourier)
    kmc, vmc = _mem_kv(mem, g_mem, w_mem_kv_b)

    vec3 = lambda g: g.reshape(DEPTH, 1, g.shape[-1])
    g_pre_mix, g_grp, g_post_mix, g_pre_ffn, g_post_ffn = map(
        vec3, (g_pre_mix, g_grp, g_post_mix, g_pre_ffn, g_post_ffn))

    in_args = (g_pre_mix, w_in_b, ab, cos_t, sin_t)
    mix = _in_proj(x, *in_args, l=0, tm=512)
    for l in range(DEPTH):
        q, k4, v4, zm, p2, q2 = mix
        ya, ym = _attention(sink, q, k4, v4, zm, kmc, vmc, g_grp, l=l, tq=512)
        yf = _seq_dft(cmat, smat, p2, q2, g_grp, l=l)
        res = _out_ffn(yf, ya, ym, x, w_out_b, g_post_mix, g_pre_ffn, w1, w2, g_post_ffn,
                       in_args if l + 1 < DEPTH else None, l=l, tm=512)
        x, mix = res[0], res[1:]
    return x
```

```python
import functools
import math

import jax
import jax.numpy as jnp
from jax import lax
from jax.experimental import pallas as pl
from jax.experimental.pallas import tpu as pltpu

D_MODEL = 1024
DEPTH = 4
HEAD_DIM = 64
FOURIER_WIDTH = 256
FOURIER_CH = 64
N_FOURIER_GROUPS = 4
ATTN_WIDTH = 512
N_Q_HEADS = 8
N_KV_HEADS = 2
KV_WIDTH = 128
MEM_WIDTH = 256
N_MEM_HEADS = 4
IN_WIDTH = 1280
WINDOW = 128
BLOCK = 128
ROPE_THETA = 10000.0
D_FF = 2816
EPS = 1e-6
NEG_INF = -1e30

LANES = 128
VMEM_LIMIT = 56 * 1024 * 1024
FF_CHUNK = 256
N_FF_CHUNKS = D_FF // FF_CHUNK
LOG2E = math.log2(math.e)
Q_SCALE = HEAD_DIM ** -0.5 * LOG2E
DFT_TILE = 512
DFT_ROWS = DFT_TILE + 16

F32 = jnp.float32
BF16 = jnp.bfloat16


def _rms(y):
    return y * lax.rsqrt(jnp.mean(y * y, axis=-1, keepdims=True) + EPS)


def _lane_low_half(shape):
    lane = lax.broadcasted_iota(jnp.int32, shape, len(shape) - 1)
    return (lane % LANES) < HEAD_DIM


def _rope_table_kernel(pos_ref, cos_ref, sin_ref):
    lane = lax.broadcasted_iota(jnp.int32, pos_ref.shape, 1)
    f = (lane % (HEAD_DIM // 2)).astype(F32)
    inv_freq = jnp.exp(-math.log(ROPE_THETA) * f * (2.0 / HEAD_DIM))
    ang = pos_ref[...] * inv_freq
    cos_ref[...] = jnp.cos(ang)
    sin_ref[...] = jnp.sin(ang)


def _rope_tables(positions):
    t = positions.size
    half = HEAD_DIM // 2
    rows = t * half // LANES
    pos = jnp.repeat(positions.reshape(t).astype(F32), half).reshape(rows, LANES)
    tr = 512
    cos_c, sin_c = pl.pallas_call(
        _rope_table_kernel,
        grid=(rows // tr,),
        in_specs=[pl.BlockSpec((tr, LANES), lambda i: (i, 0))],
        out_specs=[pl.BlockSpec((tr, LANES), lambda i: (i, 0))] * 2,
        out_shape=[jax.ShapeDtypeStruct((rows, LANES), F32)] * 2,
        name="rope_tables",
    )(pos)
    cos32 = cos_c.reshape(t, half)
    sin32 = sin_c.reshape(t, half)
    cos_t = jnp.tile(cos32, (1, 4))
    sin_t = jnp.tile(jnp.concatenate([-sin32, sin32], axis=1), (1, 2))
    return cos_t, sin_t


def _dft_matrix_kernel(c_ref, s_ref, ac_ref, as_ref, bc_ref, bs_ref, *, n):
    t = pl.program_id(0)
    grp = ac_ref.shape[0]

    @pl.when(t == 0)
    def _():
        r = lax.broadcasted_iota(jnp.int32, ac_ref.shape, 0)
        k = lax.broadcasted_iota(jnp.int32, ac_ref.shape, 1)
        a = (((r * k) & (grp - 1)).astype(F32)) * (2.0 * math.pi / grp)
        b = (((r * k) & (n - 1)).astype(F32)) * (2.0 * math.pi / n)
        ac_ref[...] = jnp.cos(a)
        as_ref[...] = jnp.sin(a)
        bc_ref[...] = jnp.cos(b)
        bs_ref[...] = jnp.sin(b)

    for s in range(pl.cdiv(DFT_ROWS, grp)):
        nrow = min(grp, DFT_ROWS - s * grp)
        j1 = t * (DFT_TILE // grp) + s
        ca = ac_ref[pl.ds(j1, 1), :]
        sa = as_ref[pl.ds(j1, 1), :]
        cb = bc_ref[:nrow, :]
        sb = bs_ref[:nrow, :]
        c_ref[s * grp:s * grp + nrow, :] = (ca * cb - sa * sb).astype(BF16)
        s_ref[s * grp:s * grp + nrow, :] = (-(sa * cb + ca * sb)).astype(BF16)


def _dft_matrices(n):
    grp = 64
    nt = n // (2 * DFT_TILE)
    assert n == grp * grp and DFT_TILE % grp == 0 and DFT_TILE < DFT_ROWS <= DFT_TILE + grp
    return pl.pallas_call(
        functools.partial(_dft_matrix_kernel, n=n),
        grid=(nt,),
        out_specs=[pl.BlockSpec((None, DFT_ROWS, n), lambda i: (i, 0, 0))] * 2,
        out_shape=[jax.ShapeDtypeStruct((nt, DFT_ROWS, n), BF16)] * 2,
        scratch_shapes=[pltpu.VMEM((grp, n), F32)] * 4,
        compiler_params=pltpu.CompilerParams(dimension_semantics=("arbitrary",), vmem_limit_bytes=VMEM_LIMIT),
        name="dft_matrices",
    )()


def _fourier_fold_kernel(w_ref, ab_ref):
    shp = (FOURIER_WIDTH, FOURIER_WIDTH)
    r = lax.broadcasted_iota(jnp.int32, shp, 0)
    c = lax.broadcasted_iota(jnp.int32, shp, 1)
    same = (r // FOURIER_CH) == (c // FOURIER_CH)
    ang = ((((r % FOURIER_CH) * (c % FOURIER_CH)) % FOURIER_CH).astype(F32)) * (2.0 * math.pi / FOURIER_CH)
    cm = jnp.where(same, jnp.cos(ang), 0.0)
    sm = jnp.where(same, jnp.sin(ang), 0.0)
    w = w_ref[...]
    a = jnp.dot(cm, w, preferred_element_type=F32, precision=lax.Precision.HIGHEST)
    b = jnp.dot(sm, w, preferred_element_type=F32, precision=lax.Precision.HIGHEST)
    ab_ref[:, :FOURIER_WIDTH] = a.astype(BF16)
    ab_ref[:, FOURIER_WIDTH:] = b.astype(BF16)


def _fourier_fold(w_fourier):
    eye = jnp.eye(N_FOURIER_GROUPS, dtype=F32)
    w_bd = jnp.einsum("lgce,gh->lgche", w_fourier, eye).reshape(DEPTH, FOURIER_WIDTH, FOURIER_WIDTH)
    return pl.pallas_call(
        _fourier_fold_kernel,
        grid=(DEPTH,),
        in_specs=[pl.BlockSpec((None, FOURIER_WIDTH, FOURIER_WIDTH), lambda l: (l, 0, 0))],
        out_specs=pl.BlockSpec((None, FOURIER_WIDTH, 2 * FOURIER_WIDTH), lambda l: (l, 0, 0)),
        out_shape=jax.ShapeDtypeStruct((DEPTH, FOURIER_WIDTH, 2 * FOURIER_WIDTH), BF16),
        name="fourier_fold",
    )(w_bd)


def _mem_kv_kernel(mem_ref, g_ref, w_ref, kc_ref, vc_ref):
    h = (_rms(mem_ref[...]) * g_ref[...]).astype(BF16)
    mkv = jnp.dot(h, w_ref[...], preferred_element_type=F32)
    low = _lane_low_half((mem_ref.shape[0], LANES))
    m = mem_ref.shape[0]
    for a in range(N_MEM_HEADS // 2):
        kp = mkv[:, a * LANES:(a + 1) * LANES]
        vp = mkv[:, MEM_WIDTH + a * LANES:MEM_WIDTH + (a + 1) * LANES]
        kc_ref[a, :m, :] = jnp.where(low, kp, 0.0).astype(BF16)
        kc_ref[a, m:, :] = jnp.where(low, 0.0, kp).astype(BF16)
        vc_ref[a, :m, :] = jnp.where(low, vp, 0.0).astype(BF16)
        vc_ref[a, m:, :] = jnp.where(low, 0.0, vp).astype(BF16)


def _mem_kv(mem, g_mem, w_mem_kv_b):
    b, m, _ = mem.shape
    npair = N_MEM_HEADS // 2
    out = jax.ShapeDtypeStruct((DEPTH, b, npair, 2 * m, LANES), BF16)
    spec = pl.BlockSpec((None, None, npair, 2 * m, LANES), lambda l, i: (l, i, 0, 0, 0))
    return pl.pallas_call(
        _mem_kv_kernel,
        grid=(DEPTH, b),
        in_specs=[
            pl.BlockSpec((None, m, D_MODEL), lambda l, i: (i, 0, 0)),
            pl.BlockSpec((None, 1, D_MODEL), lambda l, i: (l, 0, 0)),
            pl.BlockSpec((None, D_MODEL, 2 * MEM_WIDTH), lambda l, i: (l, 0, 0)),
        ],
        out_specs=[spec, spec],
        out_shape=[out, out],
        name="mem_kv",
    )(mem, g_mem.reshape(DEPTH, 1, D_MODEL), w_mem_kv_b)


def _in_proj_body(xv, g_ref, w_ref, ab_ref, cos_ref, sin_ref, q_ref, k_ref, v_ref, zm_ref, p_ref, qq_ref):
    h = (_rms(xv) * g_ref[...]).astype(BF16)
    z = jnp.dot(h, w_ref[...], preferred_element_type=F32)
    tm = z.shape[0]

    zf = z[:, :FOURIER_WIDTH].astype(BF16)
    pq = jnp.dot(zf, ab_ref[...], preferred_element_type=F32)
    p_ref[...] = pq[:, :FOURIER_WIDTH].astype(BF16)
    qq_ref[...] = pq[:, FOURIER_WIDTH:].astype(BF16)

    cos = cos_ref[...]
    sin = sin_ref[...]
    first = (lax.broadcasted_iota(jnp.int32, (tm, LANES), 1) % HEAD_DIM) < (HEAD_DIM // 2)

    def rope(t, c, s):
        rot = jnp.where(first, pltpu.roll(t, LANES - HEAD_DIM // 2, 1), pltpu.roll(t, HEAD_DIM // 2, 1))
        return t * c + rot * s

    q0 = FOURIER_WIDTH
    cq = cos * Q_SCALE
    sq = sin * Q_SCALE
    for c in range(ATTN_WIDTH // LANES):
        t = z[:, q0 + c * LANES:q0 + (c + 1) * LANES]
        q_ref[:, c * LANES:(c + 1) * LANES] = rope(t, cq, sq).astype(BF16)

    k0 = q0 + ATTN_WIDTH
    low = _lane_low_half((tm, LANES))
    kk = rope(z[:, k0:k0 + KV_WIDTH], cos, sin)
    vv = z[:, k0 + KV_WIDTH:k0 + 2 * KV_WIDTH]
    for src, dst in ((kk, k_ref), (vv, v_ref)):
        sw = pltpu.roll(src, HEAD_DIM, 1)
        dst[:, 0 * LANES:1 * LANES] = jnp.where(low, src, 0.0).astype(BF16)
        dst[:, 1 * LANES:2 * LANES] = jnp.where(low, 0.0, sw).astype(BF16)
        dst[:, 2 * LANES:3 * LANES] = jnp.where(low, sw, 0.0).astype(BF16)
        dst[:, 3 * LANES:4 * LANES] = jnp.where(low, 0.0, src).astype(BF16)

    m0 = k0 + 2 * KV_WIDTH
    zm_ref[...] = (z[:, m0:m0 + MEM_WIDTH] * Q_SCALE).astype(BF16)


def _in_proj_kernel(x_ref, *refs):
    _in_proj_body(x_ref[...], *refs)


def _in_proj_specs(l, b, s, tm):
    nt = s // tm
    tok = lambda w: pl.BlockSpec((None, tm, w), lambda i, j: (i, j, 0))
    seq_major = pl.BlockSpec((tm, FOURIER_WIDTH), lambda i, j: (j, i))
    rope_spec = pl.BlockSpec((tm, LANES), lambda i, j: (i * nt + j, 0))
    in_specs = [
        pl.BlockSpec((None, 1, D_MODEL), lambda i, j: (l, 0, 0)),
        pl.BlockSpec((None, D_MODEL, IN_WIDTH), lambda i, j: (l, 0, 0), pipeline_mode=pl.Buffered(1)),
        pl.BlockSpec((None, FOURIER_WIDTH, 2 * FOURIER_WIDTH), lambda i, j: (l, 0, 0),
                     pipeline_mode=pl.Buffered(1)),
        rope_spec, rope_spec,
    ]
    out_specs = [tok(ATTN_WIDTH), tok(4 * LANES), tok(4 * LANES), tok(MEM_WIDTH), seq_major, seq_major]
    out_shape = [
        jax.ShapeDtypeStruct((b, s, ATTN_WIDTH), BF16),
        jax.ShapeDtypeStruct((b, s, 4 * LANES), BF16),
        jax.ShapeDtypeStruct((b, s, 4 * LANES), BF16),
        jax.ShapeDtypeStruct((b, s, MEM_WIDTH), BF16),
        jax.ShapeDtypeStruct((s, b * FOURIER_WIDTH), BF16),
        jax.ShapeDtypeStruct((s, b * FOURIER_WIDTH), BF16),
    ]
    return in_specs, out_specs, out_shape


def _in_proj(x, g_pre_mix, w_in_b, ab, cos_t, sin_t, *, l, tm):
    b, s, _ = x.shape
    in_specs, out_specs, out_shape = _in_proj_specs(l, b, s, tm)
    return pl.pallas_call(
        _in_proj_kernel,
        grid=(b, s // tm),
        in_specs=[pl.BlockSpec((None, tm, D_MODEL), lambda i, j: (i, j, 0))] + in_specs,
        out_specs=out_specs,
        out_shape=out_shape,
        compiler_params=pltpu.CompilerParams(
            dimension_semantics=("arbitrary", "arbitrary"), vmem_limit_bytes=VMEM_LIMIT),
        name="in_proj",
    )(x, g_pre_mix, w_in_b, ab, cos_t, sin_t)


def _softmax_parts(sc, sink):
    m = jnp.max(sc, axis=-1, keepdims=True)
    if sink is None:
        return jnp.exp2(sc - m).astype(BF16), None
    m = jnp.maximum(m, sink)
    return jnp.exp2(sc - m).astype(BF16), jnp.exp2(sink - m)


def _head_indicator(rows):
    low = _lane_low_half((rows, LANES))
    top = jnp.where(low, 1.0, 0.0).astype(BF16)
    bot = jnp.where(low, 0.0, 1.0).astype(BF16)
    return jnp.concatenate([top, bot], axis=0)


def _attn_kernel(sink_ref, q_ref, k_ref, v_ref, zm_ref, kmc_ref, vmc_ref, gg_ref,
                 ya_ref, ym_ref, *, seq, layer):
    ga = gg_ref[:, FOURIER_WIDTH:FOURIER_WIDTH + ATTN_WIDTH]
    gm = gg_ref[:, FOURIER_WIDTH + ATTN_WIDTH:]
    tq = q_ref.shape[0]
    nblk = tq // BLOCK
    band = 3 * BLOCK
    t = pl.program_id(1)
    rel = (lax.broadcasted_iota(jnp.int32, (BLOCK, band), 0)
           - lax.broadcasted_iota(jnp.int32, (BLOCK, band), 1))
    low = _lane_low_half((BLOCK, LANES))
    pairs_per_kv = N_Q_HEADS // N_KV_HEADS // 2
    ind_band = _head_indicator(band)

    for jj in range(nblk):
        blk = t * nblk + jj
        s0 = pl.multiple_of(jnp.clip((blk - 1) * BLOCK, 0, seq - band), BLOCK)
        d = blk * BLOCK - s0
        valid = jnp.abs(rel + d) <= WINDOW
        rows = pl.ds(jj * BLOCK, BLOCK)
        outs = []
        for g in range(N_KV_HEADS):
            kcat = jnp.concatenate(
                [k_ref[pl.ds(s0, band), (2 * g) * LANES:(2 * g + 1) * LANES],
                 k_ref[pl.ds(s0, band), (2 * g + 1) * LANES:(2 * g + 2) * LANES]], axis=0)
            vcat = jnp.concatenate(
                [v_ref[pl.ds(s0, band), (2 * g) * LANES:(2 * g + 1) * LANES],
                 v_ref[pl.ds(s0, band), (2 * g + 1) * LANES:(2 * g + 2) * LANES]], axis=0)
            qg = jnp.concatenate(
                [q_ref[rows, (pairs_per_kv * g + p) * LANES:(pairs_per_kv * g + p + 1) * LANES]
                 for p in range(pairs_per_kv)], axis=0)
            sc = lax.dot_general(qg, kcat, (((1,), (1,)), ((), ())), preferred_element_type=F32)
            probs, sink_terms = [], []
            for p in range(pairs_per_kv):
                row_p, row_s = [], []
                for e in range(2):
                    head = (N_Q_HEADS // N_KV_HEADS) * g + 2 * p + e
                    sb = sc[p * BLOCK:(p + 1) * BLOCK, e * band:(e + 1) * band]
                    sb = jnp.where(valid, sb, NEG_INF)
                    pe, st = _softmax_parts(sb, sink_ref[layer, head] * LOG2E)
                    row_p.append(pe)
                    row_s.append(st)
                probs.append(jnp.concatenate(row_p, axis=1))
                sink_terms.append(row_s)
            pm = jnp.concatenate(probs, axis=0)
            o = jnp.dot(pm, jnp.concatenate([vcat, ind_band], axis=1), preferred_element_type=F32)
            for p in range(pairs_per_kv):
                op = o[p * BLOCK:(p + 1) * BLOCK, :]
                den = op[:, LANES:] + jnp.where(low, sink_terms[p][0], sink_terms[p][1])
                outs.append(op[:, :LANES] / den)
        y = jnp.concatenate(outs, axis=1)
        ya_ref[rows, :] = (_rms(y) * ga).astype(BF16)

    nmem = kmc_ref.shape[1] // 2
    ind_mem = _head_indicator(nmem)
    outs = []
    for a in range(N_MEM_HEADS // 2):
        qa = zm_ref[:, a * LANES:(a + 1) * LANES]
        sc = lax.dot_general(qa, kmc_ref[a], (((1,), (1,)), ((), ())), preferred_element_type=F32)
        p0, _ = _softmax_parts(sc[:, :nmem], None)
        p1, _ = _softmax_parts(sc[:, nmem:], None)
        o = jnp.dot(jnp.concatenate([p0, p1], axis=1), jnp.concatenate([vmc_ref[a], ind_mem], axis=1),
                    preferred_element_type=F32)
        outs.append(o[:, :LANES] / o[:, LANES:])
    ym = jnp.concatenate(outs, axis=1)
    ym_ref[...] = (_rms(ym) * gm).astype(BF16)


def _attention(sink, q, k4, v4, zm, kmc, vmc, g_grp, *, l, tq):
    b, s, _ = q.shape
    nt = s // tq
    npair, m2, _ = kmc.shape[2:]
    tok = lambda w: pl.BlockSpec((None, tq, w), lambda i, j: (i, j, 0))
    per_batch = lambda w: pl.BlockSpec((None, s, w), lambda i, j: (i, 0, 0))
    memspec = pl.BlockSpec((None, None, npair, m2, LANES), lambda i, j: (l, i, 0, 0, 0))
    return pl.pallas_call(
        functools.partial(_attn_kernel, seq=s, layer=l),
        grid=(b, nt),
        in_specs=[
            pl.BlockSpec(memory_space=pltpu.SMEM),
            tok(ATTN_WIDTH), per_batch(4 * LANES), per_batch(4 * LANES), tok(MEM_WIDTH),
            memspec, memspec,
            pl.BlockSpec((None, 1, D_MODEL), lambda i, j: (l, 0, 0)),
        ],
        out_specs=[tok(ATTN_WIDTH), tok(MEM_WIDTH)],
        out_shape=[jax.ShapeDtypeStruct((b, s, ATTN_WIDTH), BF16),
                   jax.ShapeDtypeStruct((b, s, MEM_WIDTH), BF16)],
        compiler_params=pltpu.CompilerParams(
            dimension_semantics=("arbitrary", "arbitrary"), vmem_limit_bytes=VMEM_LIMIT),
        name="attention",
    )(sink, q, k4, v4, zm, kmc, vmc, g_grp)


def _seq_dft_kernel(c_ref, s_ref, p_ref, q_ref, g_ref, y_ref, hi_ref, *, scale):
    i = pl.program_id(0)
    nt = hi_ref.shape[0]

    def group_norm(y):
        nb = y.shape[1] // FOURIER_WIDTH
        return jnp.concatenate(
            [(_rms(y[:, b * FOURIER_WIDTH:(b + 1) * FOURIER_WIDTH]) * g_ref[:, :FOURIER_WIDTH]).astype(BF16)
             for b in range(nb)], axis=1)

    @pl.when(i < nt)
    def _():
        a1 = jnp.dot(c_ref[...], p_ref[...], preferred_element_type=F32) * scale
        a2 = jnp.dot(s_ref[...], q_ref[...], preferred_element_type=F32) * scale
        y_ref[...] = group_norm((a1 + a2)[:DFT_TILE])
        hi = group_norm(a1 - a2)
        ra = lax.broadcasted_iota(jnp.int32, (DFT_TILE, DFT_TILE), 0)
        rr = lax.broadcasted_iota(jnp.int32, (DFT_TILE, DFT_TILE), 1)
        flip = jnp.where(rr == DFT_TILE - ra, 1.0, 0.0).astype(BF16)
        mirrored = jnp.dot(flip, hi[:DFT_TILE], preferred_element_type=F32).astype(BF16)
        first = lax.broadcasted_iota(jnp.int32, mirrored.shape, 0) == 0
        hi_ref[i] = jnp.where(first, hi[DFT_TILE:DFT_TILE + 1], mirrored)

    @pl.when(i >= nt)
    def _():
        y_ref[...] = hi_ref[2 * nt - 1 - i]


def _seq_dft(cmat, smat, p2, q2, g_grp, *, l):
    s, w = p2.shape
    nt = cmat.shape[0]
    scale = 1.0 / math.sqrt(s * FOURIER_CH)
    resident = pl.BlockSpec((s, w), lambda i: (0, 0), pipeline_mode=pl.Buffered(1))
    half = pl.BlockSpec((None, DFT_ROWS, s), lambda i: (jnp.minimum(i, nt - 1), 0, 0))
    return pl.pallas_call(
        functools.partial(_seq_dft_kernel, scale=scale),
        grid=(2 * nt,),
        in_specs=[half, half, resident, resident, pl.BlockSpec((None, 1, D_MODEL), lambda i: (l, 0, 0))],
        out_specs=pl.BlockSpec((DFT_TILE, w), lambda i: (i, 0)),
        out_shape=jax.ShapeDtypeStruct((s, w), BF16),
        scratch_shapes=[pltpu.VMEM((nt, DFT_TILE, w), BF16)],
        compiler_params=pltpu.CompilerParams(
            dimension_semantics=("arbitrary",), vmem_limit_bytes=VMEM_LIMIT),
        name="seq_dft",
    )(cmat, smat, p2, q2, g_grp)


def _out_ffn_kernel(yf_ref, ya_ref, ym_ref, x_ref, wo_ref, gpm_ref, gpf_ref, w1_ref, w2_ref, gpo_ref,
                    *rest, fuse_next):
    if fuse_next:
        next_in, xo_ref, next_out = rest[:5], rest[5], rest[6:12]
    else:
        xo_ref = rest[0]
    x1_ref, h_ref, acc_ref = rest[-3:]
    ycat = jnp.concatenate([yf_ref[...], ya_ref[...], ym_ref[...]], axis=1)
    y = jnp.dot(ycat, wo_ref[...], preferred_element_type=F32)
    x1 = x_ref[...] + _rms(y) * gpm_ref[...]
    x1_ref[...] = x1
    h_ref[...] = (_rms(x1) * gpf_ref[...]).astype(BF16)
    for c in range(N_FF_CHUNKS):
        cols = slice(c * FF_CHUNK, (c + 1) * FF_CHUNK)
        gate = jnp.dot(h_ref[...], w1_ref[:, cols], preferred_element_type=F32)
        up = jnp.dot(h_ref[...], w1_ref[:, D_FF + c * FF_CHUNK:D_FF + (c + 1) * FF_CHUNK],
                     preferred_element_type=F32)
        f = (gate * jax.nn.sigmoid(gate) * up).astype(BF16)
        part = jnp.dot(f, w2_ref[cols, :], preferred_element_type=F32)
        if c == 0:
            acc_ref[...] = part
        else:
            acc_ref[...] += part
    x2 = x1_ref[...] + _rms(acc_ref[...]) * gpo_ref[...]
    xo_ref[...] = x2
    if fuse_next:
        _in_proj_body(x2, *next_in, *next_out)


def _out_ffn(yf, ya, ym, x, w_out_b, g_post_mix, g_pre_ffn, w1, w2, g_post_ffn, next_args, *, l, tm):
    b, s, _ = x.shape
    nt = s // tm
    tok = lambda w: pl.BlockSpec((None, tm, w), lambda i, j: (i, j, 0))
    vec = pl.BlockSpec((None, 1, D_MODEL), lambda i, j: (l, 0, 0))
    one = pl.Buffered(1)
    in_specs = [
        pl.BlockSpec((tm, FOURIER_WIDTH), lambda i, j: (j, i)),
        tok(ATTN_WIDTH), tok(MEM_WIDTH), tok(D_MODEL),
        pl.BlockSpec((None, D_MODEL, D_MODEL), lambda i, j: (l, 0, 0), pipeline_mode=one),
        vec, vec,
        pl.BlockSpec((None, D_MODEL, 2 * D_FF), lambda i, j: (l, 0, 0), pipeline_mode=one),
        pl.BlockSpec((None, D_FF, D_MODEL), lambda i, j: (l, 0, 0), pipeline_mode=one),
        vec,
    ]
    out_specs = [tok(D_MODEL)]
    out_shape = [jax.ShapeDtypeStruct((b, s, D_MODEL), F32)]
    fuse_next = next_args is not None
    if fuse_next:
        nin, nout, nshape = _in_proj_specs(l + 1, b, s, tm)
        in_specs += nin
        out_specs += nout
        out_shape += nshape
    return pl.pallas_call(
        functools.partial(_out_ffn_kernel, fuse_next=fuse_next),
        grid=(b, nt),
        in_specs=in_specs,
        out_specs=out_specs,
        out_shape=out_shape,
        scratch_shapes=[pltpu.VMEM((tm, D_MODEL), F32), pltpu.VMEM((tm, D_MODEL), BF16),
                        pltpu.VMEM((tm, D_MODEL), F32)],
        compiler_params=pltpu.CompilerParams(
            dimension_semantics=("arbitrary", "arbitrary"), vmem_limit_bytes=VMEM_LIMIT),
        name="out_ffn",
    )(yf, ya, ym, x, w_out_b, g_post_mix, g_pre_ffn, w1, w2, g_post_ffn, *(next_args or ()))


def kernel(x, mem, positions, g_pre_mix, w_in, w_fourier, sink, g_mem, w_mem_kv, g_grp,
           w_out, g_post_mix, g_pre_ffn, w_ffn_in, w_ffn_out, g_post_ffn):
    b, s, _ = x.shape
    assert s % 512 == 0 and s >= 3 * BLOCK and s == 64 * 64

    w_in_b = w_in.astype(BF16)
    w_out_b = w_out.astype(BF16)
    w_mem_kv_b = w_mem_kv.astype(BF16)
    w1 = w_ffn_in.astype(BF16)
    w2 = w_ffn_out.astype(BF16)

    cos_t, sin_t = _rope_tables(positions)
    cmat, smat = _dft_matrices(s)
    ab = _fourier_fold(w_fourier)
    kmc, vmc = _mem_kv(mem, g_mem, w_mem_kv_b)

    vec3 = lambda g: g.reshape(DEPTH, 1, g.shape[-1])
    g_pre_mix, g_grp, g_post_mix, g_pre_ffn, g_post_ffn = map(
        vec3, (g_pre_mix, g_grp, g_post_mix, g_pre_ffn, g_post_ffn))

    in_args = (g_pre_mix, w_in_b, ab, cos_t, sin_t)
    mix = _in_proj(x, *in_args, l=0, tm=512)
    for l in range(DEPTH):
        q, k4, v4, zm, p2, q2 = mix
        ya, ym = _attention(sink, q, k4, v4, zm, kmc, vmc, g_grp, l=l, tq=1024)
        yf = _seq_dft(cmat, smat, p2, q2, g_grp, l=l)
        res = _out_ffn(yf, ya, ym, x, w_out_b, g_post_mix, g_pre_ffn, w1, w2, g_post_ffn,
                       in_args if l + 1 < DEPTH else None, l=l, tm=512)
        x, mix = res[0], res[1:]
    return x
```

```python
import functools
import math

import jax
import jax.numpy as jnp
from jax import lax
from jax.experimental import pallas as pl
from jax.experimental.pallas import tpu as pltpu

D_MODEL = 1024
DEPTH = 4
HEAD_DIM = 64
FOURIER_WIDTH = 256
FOURIER_CH = 64
N_FOURIER_GROUPS = 4
ATTN_WIDTH = 512
N_Q_HEADS = 8
N_KV_HEADS = 2
KV_WIDTH = 128
MEM_WIDTH = 256
N_MEM_HEADS = 4
IN_WIDTH = 1280
WINDOW = 128
BLOCK = 128
ROPE_THETA = 10000.0
D_FF = 2816
EPS = 1e-6
NEG_INF = -1e30

LANES = 128
VMEM_LIMIT = 56 * 1024 * 1024
FF_CHUNK = 256
N_FF_CHUNKS = D_FF // FF_CHUNK
LOG2E = math.log2(math.e)
Q_SCALE = HEAD_DIM ** -0.5 * LOG2E
DFT_TILE = 512
DFT_ROWS = DFT_TILE + 16
PAIR_TILE = 256

F32 = jnp.float32
BF16 = jnp.bfloat16


def _rms(y):
    return y * lax.rsqrt(jnp.mean(y * y, axis=-1, keepdims=True) + EPS)


def _lane_low_half(shape):
    lane = lax.broadcasted_iota(jnp.int32, shape, len(shape) - 1)
    return (lane % LANES) < HEAD_DIM


def _rope_table_kernel(pos_ref, cos_ref, sin_ref):
    lane = lax.broadcasted_iota(jnp.int32, pos_ref.shape, 1)
    f = (lane % (HEAD_DIM // 2)).astype(F32)
    inv_freq = jnp.exp(-math.log(ROPE_THETA) * f * (2.0 / HEAD_DIM))
    ang = pos_ref[...] * inv_freq
    cos_ref[...] = jnp.cos(ang)
    sin_ref[...] = jnp.sin(ang)


def _rope_tables(positions):
    t = positions.size
    half = HEAD_DIM // 2
    rows = t * half // LANES
    pos = jnp.repeat(positions.reshape(t).astype(F32), half).reshape(rows, LANES)
    tr = 512
    cos_c, sin_c = pl.pallas_call(
        _rope_table_kernel,
        grid=(rows // tr,),
        in_specs=[pl.BlockSpec((tr, LANES), lambda i: (i, 0))],
        out_specs=[pl.BlockSpec((tr, LANES), lambda i: (i, 0))] * 2,
        out_shape=[jax.ShapeDtypeStruct((rows, LANES), F32)] * 2,
        name="rope_tables",
    )(pos)
    cos32 = cos_c.reshape(t, half)
    sin32 = sin_c.reshape(t, half)
    cos_t = jnp.tile(cos32, (1, 4))
    sin_t = jnp.tile(jnp.concatenate([-sin32, sin32], axis=1), (1, 2))
    return cos_t, sin_t


def _dft_matrix_kernel(c_ref, s_ref, ac_ref, as_ref, bc_ref, bs_ref, *, n):
    t = pl.program_id(0)
    grp = ac_ref.shape[0]

    @pl.when(t == 0)
    def _():
        r = lax.broadcasted_iota(jnp.int32, ac_ref.shape, 0)
        k = lax.broadcasted_iota(jnp.int32, ac_ref.shape, 1)
        a = (((r * k) & (grp - 1)).astype(F32)) * (2.0 * math.pi / grp)
        b = (((r * k) & (n - 1)).astype(F32)) * (2.0 * math.pi / n)
        ac_ref[...] = jnp.cos(a)
        as_ref[...] = jnp.sin(a)
        bc_ref[...] = jnp.cos(b)
        bs_ref[...] = jnp.sin(b)

    for s in range(pl.cdiv(DFT_ROWS, grp)):
        nrow = min(grp, DFT_ROWS - s * grp)
        j1 = t * (DFT_TILE // grp) + s
        ca = ac_ref[pl.ds(j1, 1), :]
        sa = as_ref[pl.ds(j1, 1), :]
        cb = bc_ref[:nrow, :]
        sb = bs_ref[:nrow, :]
        c_ref[s * grp:s * grp + nrow, :] = (ca * cb - sa * sb).astype(BF16)
        s_ref[s * grp:s * grp + nrow, :] = (-(sa * cb + ca * sb)).astype(BF16)


def _dft_matrices(n):
    grp = 64
    nt = n // (2 * DFT_TILE)
    assert n == grp * grp and DFT_TILE % grp == 0 and DFT_TILE < DFT_ROWS <= DFT_TILE + grp
    return pl.pallas_call(
        functools.partial(_dft_matrix_kernel, n=n),
        grid=(nt,),
        out_specs=[pl.BlockSpec((None, DFT_ROWS, n // 2), lambda i: (i, 0, 0))] * 2,
        out_shape=[jax.ShapeDtypeStruct((nt, DFT_ROWS, n // 2), BF16)] * 2,
        scratch_shapes=[pltpu.VMEM((grp, n // 2), F32)] * 4,
        compiler_params=pltpu.CompilerParams(dimension_semantics=("arbitrary",), vmem_limit_bytes=VMEM_LIMIT),
        name="dft_matrices",
    )()


def _fourier_fold_kernel(w_ref, ab_ref):
    shp = (FOURIER_WIDTH, FOURIER_WIDTH)
    r = lax.broadcasted_iota(jnp.int32, shp, 0)
    c = lax.broadcasted_iota(jnp.int32, shp, 1)
    same = (r // FOURIER_CH) == (c // FOURIER_CH)
    ang = ((((r % FOURIER_CH) * (c % FOURIER_CH)) % FOURIER_CH).astype(F32)) * (2.0 * math.pi / FOURIER_CH)
    cm = jnp.where(same, jnp.cos(ang), 0.0)
    sm = jnp.where(same, jnp.sin(ang), 0.0)
    w = w_ref[...]
    a = jnp.dot(cm, w, preferred_element_type=F32, precision=lax.Precision.HIGHEST)
    b = jnp.dot(sm, w, preferred_element_type=F32, precision=lax.Precision.HIGHEST)
    ab_ref[:, :FOURIER_WIDTH] = a.astype(BF16)
    ab_ref[:, FOURIER_WIDTH:] = b.astype(BF16)


def _fourier_fold(w_fourier):
    eye = jnp.eye(N_FOURIER_GROUPS, dtype=F32)
    w_bd = jnp.einsum("lgce,gh->lgche", w_fourier, eye).reshape(DEPTH, FOURIER_WIDTH, FOURIER_WIDTH)
    return pl.pallas_call(
        _fourier_fold_kernel,
        grid=(DEPTH,),
        in_specs=[pl.BlockSpec((None, FOURIER_WIDTH, FOURIER_WIDTH), lambda l: (l, 0, 0))],
        out_specs=pl.BlockSpec((None, FOURIER_WIDTH, 2 * FOURIER_WIDTH), lambda l: (l, 0, 0)),
        out_shape=jax.ShapeDtypeStruct((DEPTH, FOURIER_WIDTH, 2 * FOURIER_WIDTH), BF16),
        name="fourier_fold",
    )(w_bd)


def _mem_kv_kernel(mem_ref, g_ref, w_ref, kc_ref, vc_ref):
    h = (_rms(mem_ref[...]) * g_ref[...]).astype(BF16)
    mkv = jnp.dot(h, w_ref[...], preferred_element_type=F32)
    low = _lane_low_half((mem_ref.shape[0], LANES))
    m = mem_ref.shape[0]
    for a in range(N_MEM_HEADS // 2):
        kp = mkv[:, a * LANES:(a + 1) * LANES]
        vp = mkv[:, MEM_WIDTH + a * LANES:MEM_WIDTH + (a + 1) * LANES]
        kc_ref[a, :m, :] = jnp.where(low, kp, 0.0).astype(BF16)
        kc_ref[a, m:, :] = jnp.where(low, 0.0, kp).astype(BF16)
        vc_ref[a, :m, :] = jnp.where(low, vp, 0.0).astype(BF16)
        vc_ref[a, m:, :] = jnp.where(low, 0.0, vp).astype(BF16)


def _mem_kv(mem, g_mem, w_mem_kv_b):
    b, m, _ = mem.shape
    npair = N_MEM_HEADS // 2
    out = jax.ShapeDtypeStruct((DEPTH, b, npair, 2 * m, LANES), BF16)
    spec = pl.BlockSpec((None, None, npair, 2 * m, LANES), lambda l, i: (l, i, 0, 0, 0))
    return pl.pallas_call(
        _mem_kv_kernel,
        grid=(DEPTH, b),
        in_specs=[
            pl.BlockSpec((None, m, D_MODEL), lambda l, i: (i, 0, 0)),
            pl.BlockSpec((None, 1, D_MODEL), lambda l, i: (l, 0, 0)),
            pl.BlockSpec((None, D_MODEL, 2 * MEM_WIDTH), lambda l, i: (l, 0, 0)),
        ],
        out_specs=[spec, spec],
        out_shape=[out, out],
        name="mem_kv",
    )(mem, g_mem.reshape(DEPTH, 1, D_MODEL), w_mem_kv_b)


def _in_proj_body(xv, g_ref, w_ref, ab_ref, cos_ref, sin_ref, q_ref, k_ref, v_ref, zm_ref, p_ref, qq_ref):
    h = (_rms(xv) * g_ref[...]).astype(BF16)
    z = jnp.dot(h, w_ref[...], preferred_element_type=F32)
    tm = z.shape[0]

    zf = z[:, :FOURIER_WIDTH].astype(BF16)
    pq = jnp.dot(zf, ab_ref[...], preferred_element_type=F32)
    p_ref[...] = pq[:, :FOURIER_WIDTH].astype(BF16)
    qq_ref[...] = pq[:, FOURIER_WIDTH:].astype(BF16)

    cos = cos_ref[...]
    sin = sin_ref[...]
    first = (lax.broadcasted_iota(jnp.int32, (tm, LANES), 1) % HEAD_DIM) < (HEAD_DIM // 2)

    def rope(t, c, s):
        rot = jnp.where(first, pltpu.roll(t, LANES - HEAD_DIM // 2, 1), pltpu.roll(t, HEAD_DIM // 2, 1))
        return t * c + rot * s

    q0 = FOURIER_WIDTH
    cq = cos * Q_SCALE
    sq = sin * Q_SCALE
    for c in range(ATTN_WIDTH // LANES):
        t = z[:, q0 + c * LANES:q0 + (c + 1) * LANES]
        q_ref[:, c * LANES:(c + 1) * LANES] = rope(t, cq, sq).astype(BF16)

    k0 = q0 + ATTN_WIDTH
    low = _lane_low_half((tm, LANES))
    kk = rope(z[:, k0:k0 + KV_WIDTH], cos, sin)
    vv = z[:, k0 + KV_WIDTH:k0 + 2 * KV_WIDTH]
    for src, dst in ((kk, k_ref), (vv, v_ref)):
        sw = pltpu.roll(src, HEAD_DIM, 1)
        dst[:, 0 * LANES:1 * LANES] = jnp.where(low, src, 0.0).astype(BF16)
        dst[:, 1 * LANES:2 * LANES] = jnp.where(low, 0.0, sw).astype(BF16)
        dst[:, 2 * LANES:3 * LANES] = jnp.where(low, sw, 0.0).astype(BF16)
        dst[:, 3 * LANES:4 * LANES] = jnp.where(low, 0.0, src).astype(BF16)

    m0 = k0 + 2 * KV_WIDTH
    zm_ref[...] = (z[:, m0:m0 + MEM_WIDTH] * Q_SCALE).astype(BF16)


def _in_proj_kernel(x_ref, *refs):
    _in_proj_body(x_ref[...], *refs)


def _in_proj_specs(l, b, s, tm):
    nt = s // tm
    tok = lambda w: pl.BlockSpec((None, tm, w), lambda i, j: (i, j, 0))
    seq_major = pl.BlockSpec((tm, FOURIER_WIDTH), lambda i, j: (j, i))
    rope_spec = pl.BlockSpec((tm, LANES), lambda i, j: (i * nt + j, 0))
    in_specs = [
        pl.BlockSpec((None, 1, D_MODEL), lambda i, j: (l, 0, 0)),
        pl.BlockSpec((None, D_MODEL, IN_WIDTH), lambda i, j: (l, 0, 0), pipeline_mode=pl.Buffered(1)),
        pl.BlockSpec((None, FOURIER_WIDTH, 2 * FOURIER_WIDTH), lambda i, j: (l, 0, 0),
                     pipeline_mode=pl.Buffered(1)),
        rope_spec, rope_spec,
    ]
    out_specs = [tok(ATTN_WIDTH), tok(4 * LANES), tok(4 * LANES), tok(MEM_WIDTH), seq_major, seq_major]
    out_shape = [
        jax.ShapeDtypeStruct((b, s, ATTN_WIDTH), BF16),
        jax.ShapeDtypeStruct((b, s, 4 * LANES), BF16),
        jax.ShapeDtypeStruct((b, s, 4 * LANES), BF16),
        jax.ShapeDtypeStruct((b, s, MEM_WIDTH), BF16),
        jax.ShapeDtypeStruct((s, b * FOURIER_WIDTH), BF16),
        jax.ShapeDtypeStruct((s, b * FOURIER_WIDTH), BF16),
    ]
    return in_specs, out_specs, out_shape


def _in_proj(x, g_pre_mix, w_in_b, ab, cos_t, sin_t, *, l, tm):
    b, s, _ = x.shape
    in_specs, out_specs, out_shape = _in_proj_specs(l, b, s, tm)
    return pl.pallas_call(
        _in_proj_kernel,
        grid=(b, s // tm),
        in_specs=[pl.BlockSpec((None, tm, D_MODEL), lambda i, j: (i, j, 0))] + in_specs,
        out_specs=out_specs,
        out_shape=out_shape,
        compiler_params=pltpu.CompilerParams(
            dimension_semantics=("arbitrary", "arbitrary"), vmem_limit_bytes=VMEM_LIMIT),
        name="in_proj",
    )(x, g_pre_mix, w_in_b, ab, cos_t, sin_t)


def _softmax_parts(sc, sink):
    m = jnp.max(sc, axis=-1, keepdims=True)
    if sink is None:
        return jnp.exp2(sc - m).astype(BF16), None
    m = jnp.maximum(m, sink)
    return jnp.exp2(sc - m).astype(BF16), jnp.exp2(sink - m)


def _head_indicator(rows):
    low = _lane_low_half((rows, LANES))
    top = jnp.where(low, 1.0, 0.0).astype(BF16)
    bot = jnp.where(low, 0.0, 1.0).astype(BF16)
    return jnp.concatenate([top, bot], axis=0)


def _attn_kernel(sink_ref, q_ref, k_ref, v_ref, zm_ref, kmc_ref, vmc_ref, gg_ref,
                 ya_ref, ym_ref, *, seq, layer):
    ga = gg_ref[:, FOURIER_WIDTH:FOURIER_WIDTH + ATTN_WIDTH]
    gm = gg_ref[:, FOURIER_WIDTH + ATTN_WIDTH:]
    tq = q_ref.shape[0]
    nblk = tq // BLOCK
    band = 3 * BLOCK
    t = pl.program_id(1)
    rel = (lax.broadcasted_iota(jnp.int32, (BLOCK, band), 0)
           - lax.broadcasted_iota(jnp.int32, (BLOCK, band), 1))
    low = _lane_low_half((BLOCK, LANES))
    pairs_per_kv = N_Q_HEADS // N_KV_HEADS // 2
    ind_band = _head_indicator(band)

    for jj in range(nblk):
        blk = t * nblk + jj
        s0 = pl.multiple_of(jnp.clip((blk - 1) * BLOCK, 0, seq - band), BLOCK)
        d = blk * BLOCK - s0
        valid = jnp.abs(rel + d) <= WINDOW
        rows = pl.ds(jj * BLOCK, BLOCK)
        outs = []
        for g in range(N_KV_HEADS):
            kcat = jnp.concatenate(
                [k_ref[pl.ds(s0, band), (2 * g) * LANES:(2 * g + 1) * LANES],
                 k_ref[pl.ds(s0, band), (2 * g + 1) * LANES:(2 * g + 2) * LANES]], axis=0)
            vcat = jnp.concatenate(
                [v_ref[pl.ds(s0, band), (2 * g) * LANES:(2 * g + 1) * LANES],
                 v_ref[pl.ds(s0, band), (2 * g + 1) * LANES:(2 * g + 2) * LANES]], axis=0)
            qg = jnp.concatenate(
                [q_ref[rows, (pairs_per_kv * g + p) * LANES:(pairs_per_kv * g + p + 1) * LANES]
                 for p in range(pairs_per_kv)], axis=0)
            sc = lax.dot_general(qg, kcat, (((1,), (1,)), ((), ())), preferred_element_type=F32)
            probs, sink_terms = [], []
            for p in range(pairs_per_kv):
                row_p, row_s = [], []
                for e in range(2):
                    head = (N_Q_HEADS // N_KV_HEADS) * g + 2 * p + e
                    sb = sc[p * BLOCK:(p + 1) * BLOCK, e * band:(e + 1) * band]
                    sb = jnp.where(valid, sb, NEG_INF)
                    pe, st = _softmax_parts(sb, sink_ref[layer, head] * LOG2E)
                    row_p.append(pe)
                    row_s.append(st)
                probs.append(jnp.concatenate(row_p, axis=1))
                sink_terms.append(row_s)
            pm = jnp.concatenate(probs, axis=0)
            o = jnp.dot(pm, jnp.concatenate([vcat, ind_band], axis=1), preferred_element_type=F32)
            for p in range(pairs_per_kv):
                op = o[p * BLOCK:(p + 1) * BLOCK, :]
                den = op[:, LANES:] + jnp.where(low, sink_terms[p][0], sink_terms[p][1])
                outs.append(op[:, :LANES] / den)
        y = jnp.concatenate(outs, axis=1)
        ya_ref[rows, :] = (_rms(y) * ga).astype(BF16)

    nmem = kmc_ref.shape[1] // 2
    ind_mem = _head_indicator(nmem)
    outs = []
    for a in range(N_MEM_HEADS // 2):
        qa = zm_ref[:, a * LANES:(a + 1) * LANES]
        sc = lax.dot_general(qa, kmc_ref[a], (((1,), (1,)), ((), ())), preferred_element_type=F32)
        p0, _ = _softmax_parts(sc[:, :nmem], None)
        p1, _ = _softmax_parts(sc[:, nmem:], None)
        o = jnp.dot(jnp.concatenate([p0, p1], axis=1), jnp.concatenate([vmc_ref[a], ind_mem], axis=1),
                    preferred_element_type=F32)
        outs.append(o[:, :LANES] / o[:, LANES:])
    ym = jnp.concatenate(outs, axis=1)
    ym_ref[...] = (_rms(ym) * gm).astype(BF16)


def _attention(sink, q, k4, v4, zm, kmc, vmc, g_grp, *, l, tq):
    b, s, _ = q.shape
    nt = s // tq
    npair, m2, _ = kmc.shape[2:]
    tok = lambda w: pl.BlockSpec((None, tq, w), lambda i, j: (i, j, 0))
    per_batch = lambda w: pl.BlockSpec((None, s, w), lambda i, j: (i, 0, 0))
    memspec = pl.BlockSpec((None, None, npair, m2, LANES), lambda i, j: (l, i, 0, 0, 0))
    return pl.pallas_call(
        functools.partial(_attn_kernel, seq=s, layer=l),
        grid=(b, nt),
        in_specs=[
            pl.BlockSpec(memory_space=pltpu.SMEM),
            tok(ATTN_WIDTH), per_batch(4 * LANES), per_batch(4 * LANES), tok(MEM_WIDTH),
            memspec, memspec,
            pl.BlockSpec((None, 1, D_MODEL), lambda i, j: (l, 0, 0)),
        ],
        out_specs=[tok(ATTN_WIDTH), tok(MEM_WIDTH)],
        out_shape=[jax.ShapeDtypeStruct((b, s, ATTN_WIDTH), BF16),
                   jax.ShapeDtypeStruct((b, s, MEM_WIDTH), BF16)],
        compiler_params=pltpu.CompilerParams(
            dimension_semantics=("arbitrary", "arbitrary"), vmem_limit_bytes=VMEM_LIMIT),
        name="attention",
    )(sink, q, k4, v4, zm, kmc, vmc, g_grp)


def _exchange_matrix(size):
    ra = lax.broadcasted_iota(jnp.int32, (size, size), 0)
    rr = lax.broadcasted_iota(jnp.int32, (size, size), 1)
    return jnp.where(rr == size - ra, 1.0, 0.0).astype(BF16)


def _seq_dft_kernel(c_ref, s_ref, p_ref, q_ref, g_ref, y_ref, hi_ref, pp_ref, qm_ref, *, scale):
    i = pl.program_id(0)
    nt = hi_ref.shape[0]
    n = p_ref.shape[0]
    half = n // 2

    @pl.when(i == 0)
    def _():
        flip = _exchange_matrix(PAIR_TILE)
        first = lax.broadcasted_iota(jnp.int32, (PAIR_TILE, p_ref.shape[1]), 0) == 0
        for a in range(half // PAIR_TILE):
            lo = a * PAIR_TILE
            src = n - lo - PAIR_TILE
            for x_ref, dst, sgn in ((p_ref, pp_ref, 1.0), (q_ref, qm_ref, -1.0)):
                rev = jnp.dot(flip, x_ref[src:src + PAIR_TILE, :], preferred_element_type=F32)
                if a > 0:
                    edge = x_ref[src + PAIR_TILE:src + PAIR_TILE + 16, :][0:1].astype(F32)
                    rev = jnp.where(first, edge, rev)
                dst[lo:lo + PAIR_TILE, :] = (x_ref[lo:lo + PAIR_TILE, :].astype(F32) + sgn * rev).astype(BF16)

    def group_norm(y):
        nb = y.shape[1] // FOURIER_WIDTH
        return jnp.concatenate(
            [(_rms(y[:, b * FOURIER_WIDTH:(b + 1) * FOURIER_WIDTH]) * g_ref[:, :FOURIER_WIDTH]).astype(BF16)
             for b in range(nb)], axis=1)

    @pl.when(i < nt)
    def _():
        mid = p_ref[half:half + 16, :][0:1].astype(F32)
        odd = (lax.broadcasted_iota(jnp.int32, (DFT_ROWS, 1), 0) & 1) == 1
        a1 = jnp.dot(c_ref[...], pp_ref[...], preferred_element_type=F32)
        a1 = (a1 + jnp.where(odd, -mid, mid)) * scale
        a2 = jnp.dot(s_ref[...], qm_ref[...], preferred_element_type=F32) * scale
        y_ref[...] = group_norm((a1 + a2)[:DFT_TILE])
        hi = group_norm(a1 - a2)
        mirrored = jnp.dot(_exchange_matrix(DFT_TILE), hi[:DFT_TILE], preferred_element_type=F32).astype(BF16)
        first = lax.broadcasted_iota(jnp.int32, mirrored.shape, 0) == 0
        hi_ref[i] = jnp.where(first, hi[DFT_TILE:DFT_TILE + 1], mirrored)

    @pl.when(i >= nt)
    def _():
        y_ref[...] = hi_ref[2 * nt - 1 - i]


def _seq_dft(cmat, smat, p2, q2, g_grp, *, l):
    s, w = p2.shape
    nt = cmat.shape[0]
    scale = 1.0 / math.sqrt(s * FOURIER_CH)
    resident = pl.BlockSpec((s, w), lambda i: (0, 0), pipeline_mode=pl.Buffered(1))
    half = pl.BlockSpec((None, DFT_ROWS, s // 2), lambda i: (jnp.minimum(i, nt - 1), 0, 0))
    return pl.pallas_call(
        functools.partial(_seq_dft_kernel, scale=scale),
        grid=(2 * nt,),
        in_specs=[half, half, resident, resident, pl.BlockSpec((None, 1, D_MODEL), lambda i: (l, 0, 0))],
        out_specs=pl.BlockSpec((DFT_TILE, w), lambda i: (i, 0)),
        out_shape=jax.ShapeDtypeStruct((s, w), BF16),
        scratch_shapes=[pltpu.VMEM((nt, DFT_TILE, w), BF16),
                        pltpu.VMEM((s // 2, w), BF16), pltpu.VMEM((s // 2, w), BF16)],
        compiler_params=pltpu.CompilerParams(
            dimension_semantics=("arbitrary",), vmem_limit_bytes=VMEM_LIMIT),
        name="seq_dft",
    )(cmat, smat, p2, q2, g_grp)


def _out_ffn_kernel(yf_ref, ya_ref, ym_ref, x_ref, wo_ref, gpm_ref, gpf_ref, w1_ref, w2_ref, gpo_ref,
                    *rest, fuse_next):
    if fuse_next:
        next_in, xo_ref, next_out = rest[:5], rest[5], rest[6:12]
    else:
        xo_ref = rest[0]
    x1_ref, h_ref, acc_ref = rest[-3:]
    ycat = jnp.concatenate([yf_ref[...], ya_ref[...], ym_ref[...]], axis=1)
    y = jnp.dot(ycat, wo_ref[...], preferred_element_type=F32)
    x1 = x_ref[...] + _rms(y) * gpm_ref[...]
    x1_ref[...] = x1
    h_ref[...] = (_rms(x1) * gpf_ref[...]).astype(BF16)
    for c in range(N_FF_CHUNKS):
        cols = slice(c * FF_CHUNK, (c + 1) * FF_CHUNK)
        gate = jnp.dot(h_ref[...], w1_ref[:, cols], preferred_element_type=F32)
        up = jnp.dot(h_ref[...], w1_ref[:, D_FF + c * FF_CHUNK:D_FF + (c + 1) * FF_CHUNK],
                     preferred_element_type=F32)
        f = (gate * jax.nn.sigmoid(gate) * up).astype(BF16)
        part = jnp.dot(f, w2_ref[cols, :], preferred_element_type=F32)
        if c == 0:
            acc_ref[...] = part
        else:
            acc_ref[...] += part
    x2 = x1_ref[...] + _rms(acc_ref[...]) * gpo_ref[...]
    xo_ref[...] = x2
    if fuse_next:
        _in_proj_body(x2, *next_in, *next_out)


def _out_ffn(yf, ya, ym, x, w_out_b, g_post_mix, g_pre_ffn, w1, w2, g_post_ffn, next_args, *, l, tm):
    b, s, _ = x.shape
    nt = s // tm
    tok = lambda w: pl.BlockSpec((None, tm, w), lambda i, j: (i, j, 0))
    vec = pl.BlockSpec((None, 1, D_MODEL), lambda i, j: (l, 0, 0))
    one = pl.Buffered(1)
    in_specs = [
        pl.BlockSpec((tm, FOURIER_WIDTH), lambda i, j: (j, i)),
        tok(ATTN_WIDTH), tok(MEM_WIDTH), tok(D_MODEL),
        pl.BlockSpec((None, D_MODEL, D_MODEL), lambda i, j: (l, 0, 0), pipeline_mode=one),
        vec, vec,
        pl.BlockSpec((None, D_MODEL, 2 * D_FF), lambda i, j: (l, 0, 0), pipeline_mode=one),
        pl.BlockSpec((None, D_FF, D_MODEL), lambda i, j: (l, 0, 0), pipeline_mode=one),
        vec,
    ]
    out_specs = [tok(D_MODEL)]
    out_shape = [jax.ShapeDtypeStruct((b, s, D_MODEL), F32)]
    fuse_next = next_args is not None
    if fuse_next:
        nin, nout, nshape = _in_proj_specs(l + 1, b, s, tm)
        in_specs += nin
        out_specs += nout
        out_shape += nshape
    return pl.pallas_call(
        functools.partial(_out_ffn_kernel, fuse_next=fuse_next),
        grid=(b, nt),
        in_specs=in_specs,
        out_specs=out_specs,
        out_shape=out_shape,
        scratch_shapes=[pltpu.VMEM((tm, D_MODEL), F32), pltpu.VMEM((tm, D_MODEL), BF16),
                        pltpu.VMEM((tm, D_MODEL), F32)],
        compiler_params=pltpu.CompilerParams(
            dimension_semantics=("arbitrary", "arbitrary"), vmem_limit_bytes=VMEM_LIMIT),
        name="out_ffn",
    )(yf, ya, ym, x, w_out_b, g_post_mix, g_pre_ffn, w1, w2, g_post_ffn, *(next_args or ()))


def kernel(x, mem, positions, g_pre_mix, w_in, w_fourier, sink, g_mem, w_mem_kv, g_grp,
           w_out, g_post_mix, g_pre_ffn, w_ffn_in, w_ffn_out, g_post_ffn):
    b, s, _ = x.shape
    assert s % 512 == 0 and s >= 3 * BLOCK and s == 64 * 64

    w_in_b = w_in.astype(BF16)
    w_out_b = w_out.astype(BF16)
    w_mem_kv_b = w_mem_kv.astype(BF16)
    w1 = w_ffn_in.astype(BF16)
    w2 = w_ffn_out.astype(BF16)

    cos_t, sin_t = _rope_tables(positions)
    cmat, smat = _dft_matrices(s)
    ab = _fourier_fold(w_fourier)
    kmc, vmc = _mem_kv(mem, g_mem, w_mem_kv_b)

    vec3 = lambda g: g.reshape(DEPTH, 1, g.shape[-1])
    g_pre_mix, g_grp, g_post_mix, g_pre_ffn, g_post_ffn = map(
        vec3, (g_pre_mix, g_grp, g_post_mix, g_pre_ffn, g_post_ffn))

    in_args = (g_pre_mix, w_in_b, ab, cos_t, sin_t)
    mix = _in_proj(x, *in_args, l=0, tm=512)
    for l in range(DEPTH):
        q, k4, v4, zm, p2, q2 = mix
        ya, ym = _attention(sink, q, k4, v4, zm, kmc, vmc, g_grp, l=l, tq=1024)
        yf = _seq_dft(cmat, smat, p2, q2, g_grp, l=l)
        res = _out_ffn(yf, ya, ym, x, w_out_b, g_post_mix, g_pre_ffn, w1, w2, g_post_ffn,
                       in_args if l + 1 < DEPTH else None, l=l, tm=512)
        x, mix = res[0], res[1:]
    return x
```

```python
import functools
import math

import jax
import jax.numpy as jnp
from jax import lax
from jax.experimental import pallas as pl
from jax.experimental.pallas import tpu as pltpu

D_MODEL = 1024
DEPTH = 4
HEAD_DIM = 64
FOURIER_WIDTH = 256
FOURIER_CH = 64
N_FOURIER_GROUPS = 4
ATTN_WIDTH = 512
N_Q_HEADS = 8
N_KV_HEADS = 2
KV_WIDTH = 128
MEM_WIDTH = 256
N_MEM_HEADS = 4
IN_WIDTH = 1280
WINDOW = 128
BLOCK = 128
ROPE_THETA = 10000.0
D_FF = 2816
EPS = 1e-6
NEG_INF = -1e30

LANES = 128
VMEM_LIMIT = 56 * 1024 * 1024
FF_CHUNK = 256
N_FF_CHUNKS = D_FF // FF_CHUNK
LOG2E = math.log2(math.e)
Q_SCALE = HEAD_DIM ** -0.5 * LOG2E
DFT_TILE = 512
DFT_ROWS = DFT_TILE + 16
PAIR_TILE = 256

F32 = jnp.float32
BF16 = jnp.bfloat16


def _rms(y):
    return y * lax.rsqrt(jnp.mean(y * y, axis=-1, keepdims=True) + EPS)


def _lane_low_half(shape):
    lane = lax.broadcasted_iota(jnp.int32, shape, len(shape) - 1)
    return (lane % LANES) < HEAD_DIM


def _rope_table_kernel(pos_ref, cos_ref, sin_ref):
    lane = lax.broadcasted_iota(jnp.int32, pos_ref.shape, 1)
    f = (lane % (HEAD_DIM // 2)).astype(F32)
    inv_freq = jnp.exp(-math.log(ROPE_THETA) * f * (2.0 / HEAD_DIM))
    ang = pos_ref[...] * inv_freq
    cos_ref[...] = jnp.cos(ang)
    sin_ref[...] = jnp.sin(ang)


def _rope_tables(positions):
    t = positions.size
    half = HEAD_DIM // 2
    rows = t * half // LANES
    pos = jnp.repeat(positions.reshape(t).astype(F32), half).reshape(rows, LANES)
    tr = 512
    cos_c, sin_c = pl.pallas_call(
        _rope_table_kernel,
        grid=(rows // tr,),
        in_specs=[pl.BlockSpec((tr, LANES), lambda i: (i, 0))],
        out_specs=[pl.BlockSpec((tr, LANES), lambda i: (i, 0))] * 2,
        out_shape=[jax.ShapeDtypeStruct((rows, LANES), F32)] * 2,
        name="rope_tables",
    )(pos)
    cos32 = cos_c.reshape(t, half)
    sin32 = sin_c.reshape(t, half)
    cos_t = jnp.tile(cos32, (1, 4))
    sin_t = jnp.tile(jnp.concatenate([-sin32, sin32], axis=1), (1, 2))
    return cos_t, sin_t


def _dft_matrix_kernel(c_ref, s_ref, ac_ref, as_ref, bc_ref, bs_ref, *, n):
    t = pl.program_id(0)
    grp = ac_ref.shape[0]

    @pl.when(t == 0)
    def _():
        r = lax.broadcasted_iota(jnp.int32, ac_ref.shape, 0)
        k = lax.broadcasted_iota(jnp.int32, ac_ref.shape, 1)
        a = (((r * k) & (grp - 1)).astype(F32)) * (2.0 * math.pi / grp)
        b = (((r * k) & (n - 1)).astype(F32)) * (2.0 * math.pi / n)
        ac_ref[...] = jnp.cos(a)
        as_ref[...] = jnp.sin(a)
        bc_ref[...] = jnp.cos(b)
        bs_ref[...] = jnp.sin(b)

    for s in range(pl.cdiv(DFT_ROWS, grp)):
        nrow = min(grp, DFT_ROWS - s * grp)
        j1 = t * (DFT_TILE // grp) + s
        ca = ac_ref[pl.ds(j1, 1), :]
        sa = as_ref[pl.ds(j1, 1), :]
        cb = bc_ref[:nrow, :]
        sb = bs_ref[:nrow, :]
        c_ref[s * grp:s * grp + nrow, :] = (ca * cb - sa * sb).astype(BF16)
        s_ref[s * grp:s * grp + nrow, :] = (-(sa * cb + ca * sb)).astype(BF16)


def _dft_matrices(n):
    grp = 64
    nt = n // (2 * DFT_TILE)
    assert n == grp * grp and DFT_TILE % grp == 0 and DFT_TILE < DFT_ROWS <= DFT_TILE + grp
    return pl.pallas_call(
        functools.partial(_dft_matrix_kernel, n=n),
        grid=(nt,),
        out_specs=[pl.BlockSpec((None, DFT_ROWS, n // 2), lambda i: (i, 0, 0))] * 2,
        out_shape=[jax.ShapeDtypeStruct((nt, DFT_ROWS, n // 2), BF16)] * 2,
        scratch_shapes=[pltpu.VMEM((grp, n // 2), F32)] * 4,
        compiler_params=pltpu.CompilerParams(dimension_semantics=("arbitrary",), vmem_limit_bytes=VMEM_LIMIT),
        name="dft_matrices",
    )()


def _fourier_fold_kernel(w_ref, ab_ref):
    shp = (FOURIER_WIDTH, FOURIER_WIDTH)
    r = lax.broadcasted_iota(jnp.int32, shp, 0)
    c = lax.broadcasted_iota(jnp.int32, shp, 1)
    same = (r // FOURIER_CH) == (c // FOURIER_CH)
    ang = ((((r % FOURIER_CH) * (c % FOURIER_CH)) % FOURIER_CH).astype(F32)) * (2.0 * math.pi / FOURIER_CH)
    cm = jnp.where(same, jnp.cos(ang), 0.0)
    sm = jnp.where(same, jnp.sin(ang), 0.0)
    w = w_ref[...]
    a = jnp.dot(cm, w, preferred_element_type=F32, precision=lax.Precision.HIGHEST)
    b = jnp.dot(sm, w, preferred_element_type=F32, precision=lax.Precision.HIGHEST)
    ab_ref[:, :FOURIER_WIDTH] = a.astype(BF16)
    ab_ref[:, FOURIER_WIDTH:] = b.astype(BF16)


def _fourier_fold(w_fourier):
    eye = jnp.eye(N_FOURIER_GROUPS, dtype=F32)
    w_bd = jnp.einsum("lgce,gh->lgche", w_fourier, eye).reshape(DEPTH, FOURIER_WIDTH, FOURIER_WIDTH)
    return pl.pallas_call(
        _fourier_fold_kernel,
        grid=(DEPTH,),
        in_specs=[pl.BlockSpec((None, FOURIER_WIDTH, FOURIER_WIDTH), lambda l: (l, 0, 0))],
        out_specs=pl.BlockSpec((None, FOURIER_WIDTH, 2 * FOURIER_WIDTH), lambda l: (l, 0, 0)),
        out_shape=jax.ShapeDtypeStruct((DEPTH, FOURIER_WIDTH, 2 * FOURIER_WIDTH), BF16),
        name="fourier_fold",
    )(w_bd)


def _mem_kv_kernel(mem_ref, g_ref, w_ref, kc_ref, vc_ref):
    h = (_rms(mem_ref[...]) * g_ref[...]).astype(BF16)
    mkv = jnp.dot(h, w_ref[...], preferred_element_type=F32)
    low = _lane_low_half((mem_ref.shape[0], LANES))
    m = mem_ref.shape[0]
    for a in range(N_MEM_HEADS // 2):
        kp = mkv[:, a * LANES:(a + 1) * LANES]
        vp = mkv[:, MEM_WIDTH + a * LANES:MEM_WIDTH + (a + 1) * LANES]
        kc_ref[a, :m, :] = jnp.where(low, kp, 0.0).astype(BF16)
        kc_ref[a, m:, :] = jnp.where(low, 0.0, kp).astype(BF16)
        vc_ref[a, :m, :] = jnp.where(low, vp, 0.0).astype(BF16)
        vc_ref[a, m:, :] = jnp.where(low, 0.0, vp).astype(BF16)


def _mem_kv(mem, g_mem, w_mem_kv_b):
    b, m, _ = mem.shape
    npair = N_MEM_HEADS // 2
    out = jax.ShapeDtypeStruct((DEPTH, b, npair, 2 * m, LANES), BF16)
    spec = pl.BlockSpec((None, None, npair, 2 * m, LANES), lambda l, i: (l, i, 0, 0, 0))
    return pl.pallas_call(
        _mem_kv_kernel,
        grid=(DEPTH, b),
        in_specs=[
            pl.BlockSpec((None, m, D_MODEL), lambda l, i: (i, 0, 0)),
            pl.BlockSpec((None, 1, D_MODEL), lambda l, i: (l, 0, 0)),
            pl.BlockSpec((None, D_MODEL, 2 * MEM_WIDTH), lambda l, i: (l, 0, 0)),
        ],
        out_specs=[spec, spec],
        out_shape=[out, out],
        name="mem_kv",
    )(mem, g_mem.reshape(DEPTH, 1, D_MODEL), w_mem_kv_b)


def _in_proj_body(xv, g_ref, w_ref, ab_ref, cos_ref, sin_ref, q_ref, k_ref, v_ref, zm_ref, p_ref, qq_ref):
    h = (_rms(xv) * g_ref[...]).astype(BF16)
    z = jnp.dot(h, w_ref[...], preferred_element_type=F32)
    tm = z.shape[0]

    zf = z[:, :FOURIER_WIDTH].astype(BF16)
    pq = jnp.dot(zf, ab_ref[...], preferred_element_type=F32)
    p_ref[...] = pq[:, :FOURIER_WIDTH].astype(BF16)
    qq_ref[...] = pq[:, FOURIER_WIDTH:].astype(BF16)

    cos = cos_ref[...]
    sin = sin_ref[...]
    first = (lax.broadcasted_iota(jnp.int32, (tm, LANES), 1) % HEAD_DIM) < (HEAD_DIM // 2)

    def rope(t, c, s):
        rot = jnp.where(first, pltpu.roll(t, LANES - HEAD_DIM // 2, 1), pltpu.roll(t, HEAD_DIM // 2, 1))
        return t * c + rot * s

    q0 = FOURIER_WIDTH
    cq = cos * Q_SCALE
    sq = sin * Q_SCALE
    for c in range(ATTN_WIDTH // LANES):
        t = z[:, q0 + c * LANES:q0 + (c + 1) * LANES]
        q_ref[:, c * LANES:(c + 1) * LANES] = rope(t, cq, sq).astype(BF16)

    k0 = q0 + ATTN_WIDTH
    low = _lane_low_half((tm, LANES))
    kk = rope(z[:, k0:k0 + KV_WIDTH], cos, sin)
    vv = z[:, k0 + KV_WIDTH:k0 + 2 * KV_WIDTH]
    for src, dst in ((kk, k_ref), (vv, v_ref)):
        sw = pltpu.roll(src, HEAD_DIM, 1)
        dst[:, 0 * LANES:1 * LANES] = jnp.where(low, src, 0.0).astype(BF16)
        dst[:, 1 * LANES:2 * LANES] = jnp.where(low, 0.0, sw).astype(BF16)
        dst[:, 2 * LANES:3 * LANES] = jnp.where(low, sw, 0.0).astype(BF16)
        dst[:, 3 * LANES:4 * LANES] = jnp.where(low, 0.0, src).astype(BF16)

    m0 = k0 + 2 * KV_WIDTH
    zm_ref[...] = (z[:, m0:m0 + MEM_WIDTH] * Q_SCALE).astype(BF16)


def _in_proj_kernel(x_ref, *refs):
    _in_proj_body(x_ref[...], *refs)


def _in_proj_specs(l, b, s, tm):
    nt = s // tm
    tok = lambda w: pl.BlockSpec((None, tm, w), lambda i, j: (i, j, 0))
    seq_major = pl.BlockSpec((tm, FOURIER_WIDTH), lambda i, j: (j, i))
    rope_spec = pl.BlockSpec((tm, LANES), lambda i, j: (i * nt + j, 0))
    in_specs = [
        pl.BlockSpec((None, 1, D_MODEL), lambda i, j: (l, 0, 0)),
        pl.BlockSpec((D_MODEL, IN_WIDTH), lambda i, j: (0, 0), pipeline_mode=pl.Buffered(1)),
        pl.BlockSpec((None, FOURIER_WIDTH, 2 * FOURIER_WIDTH), lambda i, j: (l, 0, 0),
                     pipeline_mode=pl.Buffered(1)),
        rope_spec, rope_spec,
    ]
    out_specs = [tok(ATTN_WIDTH), tok(4 * LANES), tok(4 * LANES), tok(MEM_WIDTH), seq_major, seq_major]
    out_shape = [
        jax.ShapeDtypeStruct((b, s, ATTN_WIDTH), BF16),
        jax.ShapeDtypeStruct((b, s, 4 * LANES), BF16),
        jax.ShapeDtypeStruct((b, s, 4 * LANES), BF16),
        jax.ShapeDtypeStruct((b, s, MEM_WIDTH), BF16),
        jax.ShapeDtypeStruct((s, b * FOURIER_WIDTH), BF16),
        jax.ShapeDtypeStruct((s, b * FOURIER_WIDTH), BF16),
    ]
    return in_specs, out_specs, out_shape


def _in_proj(x, g_pre_mix, w_in_b, ab, cos_t, sin_t, *, l, tm):
    b, s, _ = x.shape
    in_specs, out_specs, out_shape = _in_proj_specs(l, b, s, tm)
    return pl.pallas_call(
        _in_proj_kernel,
        grid=(b, s // tm),
        in_specs=[pl.BlockSpec((None, tm, D_MODEL), lambda i, j: (i, j, 0))] + in_specs,
        out_specs=out_specs,
        out_shape=out_shape,
        compiler_params=pltpu.CompilerParams(
            dimension_semantics=("arbitrary", "arbitrary"), vmem_limit_bytes=VMEM_LIMIT),
        name="in_proj",
    )(x, g_pre_mix, w_in_b, ab, cos_t, sin_t)


def _softmax_parts(sc, sink):
    m = jnp.max(sc, axis=-1, keepdims=True)
    if sink is None:
        return jnp.exp2(sc - m).astype(BF16), None
    m = jnp.maximum(m, sink)
    return jnp.exp2(sc - m).astype(BF16), jnp.exp2(sink - m)


def _head_indicator(rows):
    low = _lane_low_half((rows, LANES))
    top = jnp.where(low, 1.0, 0.0).astype(BF16)
    bot = jnp.where(low, 0.0, 1.0).astype(BF16)
    return jnp.concatenate([top, bot], axis=0)


def _attn_kernel(sink_ref, q_ref, k_ref, v_ref, zm_ref, kmc_ref, vmc_ref, gg_ref,
                 ya_ref, ym_ref, *, seq, layer):
    ga = gg_ref[:, FOURIER_WIDTH:FOURIER_WIDTH + ATTN_WIDTH]
    gm = gg_ref[:, FOURIER_WIDTH + ATTN_WIDTH:]
    tq = q_ref.shape[0]
    nblk = tq // BLOCK
    band = 3 * BLOCK
    t = pl.program_id(1)
    rel = (lax.broadcasted_iota(jnp.int32, (BLOCK, band), 0)
           - lax.broadcasted_iota(jnp.int32, (BLOCK, band), 1))
    low = _lane_low_half((BLOCK, LANES))
    pairs_per_kv = N_Q_HEADS // N_KV_HEADS // 2
    ind_band = _head_indicator(band)

    for jj in range(nblk):
        blk = t * nblk + jj
        s0 = pl.multiple_of(jnp.clip((blk - 1) * BLOCK, 0, seq - band), BLOCK)
        d = blk * BLOCK - s0
        valid = jnp.abs(rel + d) <= WINDOW
        rows = pl.ds(jj * BLOCK, BLOCK)
        outs = []
        for g in range(N_KV_HEADS):
            kcat = jnp.concatenate(
                [k_ref[pl.ds(s0, band), (2 * g) * LANES:(2 * g + 1) * LANES],
                 k_ref[pl.ds(s0, band), (2 * g + 1) * LANES:(2 * g + 2) * LANES]], axis=0)
            vcat = jnp.concatenate(
                [v_ref[pl.ds(s0, band), (2 * g) * LANES:(2 * g + 1) * LANES],
                 v_ref[pl.ds(s0, band), (2 * g + 1) * LANES:(2 * g + 2) * LANES]], axis=0)
            qg = jnp.concatenate(
                [q_ref[rows, (pairs_per_kv * g + p) * LANES:(pairs_per_kv * g + p + 1) * LANES]
                 for p in range(pairs_per_kv)], axis=0)
            sc = lax.dot_general(qg, kcat, (((1,), (1,)), ((), ())), preferred_element_type=F32)
            probs, sink_terms = [], []
            for p in range(pairs_per_kv):
                row_p, row_s = [], []
                for e in range(2):
                    head = (N_Q_HEADS // N_KV_HEADS) * g + 2 * p + e
                    sb = sc[p * BLOCK:(p + 1) * BLOCK, e * band:(e + 1) * band]
                    sb = jnp.where(valid, sb, NEG_INF)
                    pe, st = _softmax_parts(sb, sink_ref[layer, head] * LOG2E)
                    row_p.append(pe)
                    row_s.append(st)
                probs.append(jnp.concatenate(row_p, axis=1))
                sink_terms.append(row_s)
            pm = jnp.concatenate(probs, axis=0)
            o = jnp.dot(pm, jnp.concatenate([vcat, ind_band], axis=1), preferred_element_type=F32)
            for p in range(pairs_per_kv):
                op = o[p * BLOCK:(p + 1) * BLOCK, :]
                den = op[:, LANES:] + jnp.where(low, sink_terms[p][0], sink_terms[p][1])
                outs.append(op[:, :LANES] / den)
        y = jnp.concatenate(outs, axis=1)
        ya_ref[rows, :] = (_rms(y) * ga).astype(BF16)

    nmem = kmc_ref.shape[1] // 2
    ind_mem = _head_indicator(nmem)
    outs = []
    for a in range(N_MEM_HEADS // 2):
        qa = zm_ref[:, a * LANES:(a + 1) * LANES]
        sc = lax.dot_general(qa, kmc_ref[a], (((1,), (1,)), ((), ())), preferred_element_type=F32)
        p0, _ = _softmax_parts(sc[:, :nmem], None)
        p1, _ = _softmax_parts(sc[:, nmem:], None)
        o = jnp.dot(jnp.concatenate([p0, p1], axis=1), jnp.concatenate([vmc_ref[a], ind_mem], axis=1),
                    preferred_element_type=F32)
        outs.append(o[:, :LANES] / o[:, LANES:])
    ym = jnp.concatenate(outs, axis=1)
    ym_ref[...] = (_rms(ym) * gm).astype(BF16)


def _attention(sink, q, k4, v4, zm, kmc, vmc, g_grp, *, l, tq):
    b, s, _ = q.shape
    nt = s // tq
    npair, m2, _ = kmc.shape[2:]
    tok = lambda w: pl.BlockSpec((None, tq, w), lambda i, j: (i, j, 0))
    per_batch = lambda w: pl.BlockSpec((None, s, w), lambda i, j: (i, 0, 0))
    memspec = pl.BlockSpec((None, None, npair, m2, LANES), lambda i, j: (l, i, 0, 0, 0))
    return pl.pallas_call(
        functools.partial(_attn_kernel, seq=s, layer=l),
        grid=(b, nt),
        in_specs=[
            pl.BlockSpec(memory_space=pltpu.SMEM),
            tok(ATTN_WIDTH), per_batch(4 * LANES), per_batch(4 * LANES), tok(MEM_WIDTH),
            memspec, memspec,
            pl.BlockSpec((None, 1, D_MODEL), lambda i, j: (l, 0, 0)),
        ],
        out_specs=[tok(ATTN_WIDTH), tok(MEM_WIDTH)],
        out_shape=[jax.ShapeDtypeStruct((b, s, ATTN_WIDTH), BF16),
                   jax.ShapeDtypeStruct((b, s, MEM_WIDTH), BF16)],
        compiler_params=pltpu.CompilerParams(
            dimension_semantics=("arbitrary", "arbitrary"), vmem_limit_bytes=VMEM_LIMIT),
        name="attention",
    )(sink, q, k4, v4, zm, kmc, vmc, g_grp)


def _exchange_matrix(size):
    ra = lax.broadcasted_iota(jnp.int32, (size, size), 0)
    rr = lax.broadcasted_iota(jnp.int32, (size, size), 1)
    return jnp.where(rr == size - ra, 1.0, 0.0).astype(BF16)


def _seq_dft_kernel(c_ref, s_ref, p_ref, q_ref, g_ref, y_ref, hi_ref, pp_ref, qm_ref, *, scale):
    i = pl.program_id(0)
    nt = hi_ref.shape[0]
    n = p_ref.shape[0]
    half = n // 2

    @pl.when(i == 0)
    def _():
        flip = _exchange_matrix(PAIR_TILE)
        first = lax.broadcasted_iota(jnp.int32, (PAIR_TILE, p_ref.shape[1]), 0) == 0
        for a in range(half // PAIR_TILE):
            lo = a * PAIR_TILE
            src = n - lo - PAIR_TILE
            for x_ref, dst, sgn in ((p_ref, pp_ref, 1.0), (q_ref, qm_ref, -1.0)):
                rev = jnp.dot(flip, x_ref[src:src + PAIR_TILE, :], preferred_element_type=F32)
                if a > 0:
                    edge = x_ref[src + PAIR_TILE:src + PAIR_TILE + 16, :][0:1].astype(F32)
                    rev = jnp.where(first, edge, rev)
                dst[lo:lo + PAIR_TILE, :] = (x_ref[lo:lo + PAIR_TILE, :].astype(F32) + sgn * rev).astype(BF16)

    def group_norm(y):
        nb = y.shape[1] // FOURIER_WIDTH
        return jnp.concatenate(
            [(_rms(y[:, b * FOURIER_WIDTH:(b + 1) * FOURIER_WIDTH]) * g_ref[:, :FOURIER_WIDTH]).astype(BF16)
             for b in range(nb)], axis=1)

    @pl.when(i < nt)
    def _():
        mid = p_ref[half:half + 16, :][0:1].astype(F32)
        odd = (lax.broadcasted_iota(jnp.int32, (DFT_ROWS, 1), 0) & 1) == 1
        a1 = jnp.dot(c_ref[...], pp_ref[...], preferred_element_type=F32)
        a1 = (a1 + jnp.where(odd, -mid, mid)) * scale
        a2 = jnp.dot(s_ref[...], qm_ref[...], preferred_element_type=F32) * scale
        y_ref[...] = group_norm((a1 + a2)[:DFT_TILE])
        hi = group_norm(a1 - a2)
        mirrored = jnp.dot(_exchange_matrix(DFT_TILE), hi[:DFT_TILE], preferred_element_type=F32).astype(BF16)
        first = lax.broadcasted_iota(jnp.int32, mirrored.shape, 0) == 0
        hi_ref[i] = jnp.where(first, hi[DFT_TILE:DFT_TILE + 1], mirrored)

    @pl.when(i >= nt)
    def _():
        y_ref[...] = hi_ref[2 * nt - 1 - i]


def _seq_dft(cmat, smat, p2, q2, g_grp, *, l):
    s, w = p2.shape
    nt = cmat.shape[0]
    scale = 1.0 / math.sqrt(s * FOURIER_CH)
    resident = pl.BlockSpec((s, w), lambda i: (0, 0), pipeline_mode=pl.Buffered(1))
    half = pl.BlockSpec((None, DFT_ROWS, s // 2), lambda i: (jnp.minimum(i, nt - 1), 0, 0))
    return pl.pallas_call(
        functools.partial(_seq_dft_kernel, scale=scale),
        grid=(2 * nt,),
        in_specs=[half, half, resident, resident, pl.BlockSpec((None, 1, D_MODEL), lambda i: (l, 0, 0))],
        out_specs=pl.BlockSpec((DFT_TILE, w), lambda i: (i, 0)),
        out_shape=jax.ShapeDtypeStruct((s, w), BF16),
        scratch_shapes=[pltpu.VMEM((nt, DFT_TILE, w), BF16),
                        pltpu.VMEM((s // 2, w), BF16), pltpu.VMEM((s // 2, w), BF16)],
        compiler_params=pltpu.CompilerParams(
            dimension_semantics=("arbitrary",), vmem_limit_bytes=VMEM_LIMIT),
        name="seq_dft",
    )(cmat, smat, p2, q2, g_grp)


def _out_ffn_kernel(yf_ref, ya_ref, ym_ref, x_ref, wo_ref, gpm_ref, gpf_ref, w1_ref, w2_ref, gpo_ref,
                    *rest, fuse_next, n_cast):
    n_next_in = 5 if fuse_next else 0
    n_next_out = 6 if fuse_next else 0
    next_in = rest[:n_next_in]
    cast_src = rest[n_next_in:n_next_in + n_cast]
    outs = rest[n_next_in + n_cast:-3]
    xo_ref, next_out, cast_dst = outs[0], outs[1:1 + n_next_out], outs[1 + n_next_out:]
    x1_ref, h_ref, acc_ref = rest[-3:]
    for src, dst in zip(cast_src, cast_dst):
        dst[...] = src[...].astype(BF16)
    ycat = jnp.concatenate([yf_ref[...], ya_ref[...], ym_ref[...]], axis=1)
    y = jnp.dot(ycat, wo_ref[...], preferred_element_type=F32)
    x1 = x_ref[...] + _rms(y) * gpm_ref[...]
    x1_ref[...] = x1
    h_ref[...] = (_rms(x1) * gpf_ref[...]).astype(BF16)
    for c in range(N_FF_CHUNKS):
        cols = slice(c * FF_CHUNK, (c + 1) * FF_CHUNK)
        gate = jnp.dot(h_ref[...], w1_ref[:, cols], preferred_element_type=F32)
        up = jnp.dot(h_ref[...], w1_ref[:, D_FF + c * FF_CHUNK:D_FF + (c + 1) * FF_CHUNK],
                     preferred_element_type=F32)
        f = (gate * jax.nn.sigmoid(gate) * up).astype(BF16)
        part = jnp.dot(f, w2_ref[cols, :], preferred_element_type=F32)
        if c == 0:
            acc_ref[...] = part
        else:
            acc_ref[...] += part
    x2 = x1_ref[...] + _rms(acc_ref[...]) * gpo_ref[...]
    xo_ref[...] = x2
    if fuse_next:
        _in_proj_body(x2, *next_in, *next_out)


def _out_ffn(yf, ya, ym, x, w_out_l, g_post_mix, g_pre_ffn, w1_l, w2_l, g_post_ffn, next_args, casts, *, l, tm):
    b, s, _ = x.shape
    nt = s // tm
    steps = b * nt
    tok = lambda w: pl.BlockSpec((None, tm, w), lambda i, j: (i, j, 0))
    vec = pl.BlockSpec((None, 1, D_MODEL), lambda i, j: (l, 0, 0))
    one = pl.Buffered(1)
    flat = lambda r, c: pl.BlockSpec((r, c), lambda i, j: (0, 0), pipeline_mode=one)
    in_specs = [
        pl.BlockSpec((tm, FOURIER_WIDTH), lambda i, j: (j, i)),
        tok(ATTN_WIDTH), tok(MEM_WIDTH), tok(D_MODEL),
        flat(D_MODEL, D_MODEL), vec, vec, flat(D_MODEL, 2 * D_FF), flat(D_FF, D_MODEL), vec,
    ]
    out_specs = [tok(D_MODEL)]
    out_shape = [jax.ShapeDtypeStruct((b, s, D_MODEL), F32)]
    fuse_next = next_args is not None
    if fuse_next:
        nin, nout, nshape = _in_proj_specs(l + 1, b, s, tm)
        in_specs += nin
        out_specs += nout
        out_shape += nshape
    for w, layer, rows in casts:
        n_rows, n_cols = w.shape[1:]
        last = n_rows // rows - 1
        assert n_rows % rows == 0 and rows % 16 == 0 and last < steps
        in_specs.append(pl.BlockSpec(
            (None, rows, n_cols), lambda i, j, layer=layer, last=last: (layer, jnp.minimum(i * nt + j, last), 0)))
        out_specs.append(pl.BlockSpec(
            (rows, n_cols), lambda i, j, last=last: (jnp.minimum(i * nt + j, last), 0)))
        out_shape.append(jax.ShapeDtypeStruct((n_rows, n_cols), BF16))
    return pl.pallas_call(
        functools.partial(_out_ffn_kernel, fuse_next=fuse_next, n_cast=len(casts)),
        grid=(b, nt),
        in_specs=in_specs,
        out_specs=out_specs,
        out_shape=out_shape,
        scratch_shapes=[pltpu.VMEM((tm, D_MODEL), F32), pltpu.VMEM((tm, D_MODEL), BF16),
                        pltpu.VMEM((tm, D_MODEL), F32)],
        compiler_params=pltpu.CompilerParams(
            dimension_semantics=("arbitrary", "arbitrary"), vmem_limit_bytes=VMEM_LIMIT),
        name="out_ffn",
    )(yf, ya, ym, x, w_out_l, g_post_mix, g_pre_ffn, w1_l, w2_l, g_post_ffn,
      *(next_args or ()), *(w for w, _, _ in casts))


def kernel(x, mem, positions, g_pre_mix, w_in, w_fourier, sink, g_mem, w_mem_kv, g_grp,
           w_out, g_post_mix, g_pre_ffn, w_ffn_in, w_ffn_out, g_post_ffn):
    b, s, _ = x.shape
    assert s % 512 == 0 and s >= 3 * BLOCK and s == 64 * 64

    w_mem_kv_b = w_mem_kv.astype(BF16)
    w_in_b = {0: w_in[0].astype(BF16), 1: w_in[1].astype(BF16)}
    w_out_b = {0: w_out[0].astype(BF16)}
    w1 = {0: w_ffn_in[0].astype(BF16)}
    w2 = {0: w_ffn_out[0].astype(BF16)}

    cos_t, sin_t = _rope_tables(positions)
    cmat, smat = _dft_matrices(s)
    ab = _fourier_fold(w_fourier)
    kmc, vmc = _mem_kv(mem, g_mem, w_mem_kv_b)

    vec3 = lambda g: g.reshape(DEPTH, 1, g.shape[-1])
    g_pre_mix, g_grp, g_post_mix, g_pre_ffn, g_post_ffn = map(
        vec3, (g_pre_mix, g_grp, g_post_mix, g_pre_ffn, g_post_ffn))

    in_args = lambda l: (g_pre_mix, w_in_b[l], ab, cos_t, sin_t)
    mix = _in_proj(x, *in_args(0), l=0, tm=1024)
    for l in range(DEPTH):
        q, k4, v4, zm, p2, q2 = mix
        ya, ym = _attention(sink, q, k4, v4, zm, kmc, vmc, g_grp, l=l, tq=1024)
        yf = _seq_dft(cmat, smat, p2, q2, g_grp, l=l)
        casts = []
        if l + 1 < DEPTH:
            casts += [(w_out, l + 1, 32), (w_ffn_in, l + 1, 32), (w_ffn_out, l + 1, 176)]
        if l + 2 < DEPTH:
            casts += [(w_in, l + 2, 32)]
        res = _out_ffn(yf, ya, ym, x, w_out_b[l], g_post_mix, g_pre_ffn, w1[l], w2[l], g_post_ffn,
                       in_args(l + 1) if l + 1 < DEPTH else None, casts, l=l, tm=512)
        x = res[0]
        if l + 1 < DEPTH:
            mix = res[1:7]
            w_out_b[l + 1], w1[l + 1], w2[l + 1] = res[7:10]
        if l + 2 < DEPTH:
            w_in_b[l + 2] = res[10]
    return x
```

```python
import functools
import math

import jax
import jax.numpy as jnp
from jax import lax
from jax.experimental import pallas as pl
from jax.experimental.pallas import tpu as pltpu

D_MODEL = 1024
DEPTH = 4
HEAD_DIM = 64
FOURIER_WIDTH = 256
FOURIER_CH = 64
N_FOURIER_GROUPS = 4
ATTN_WIDTH = 512
N_Q_HEADS = 8
N_KV_HEADS = 2
KV_WIDTH = 128
MEM_WIDTH = 256
N_MEM_HEADS = 4
IN_WIDTH = 1280
WINDOW = 128
BLOCK = 128
ROPE_THETA = 10000.0
D_FF = 2816
EPS = 1e-6
NEG_INF = -1e30

LANES = 128
VMEM_LIMIT = 56 * 1024 * 1024
FF_CHUNK = 256
N_FF_CHUNKS = D_FF // FF_CHUNK
LOG2E = math.log2(math.e)
Q_SCALE = HEAD_DIM ** -0.5 * LOG2E
DFT_TILE = 512
DFT_ROWS = DFT_TILE + 16
PAIR_TILE = 256

F32 = jnp.float32
BF16 = jnp.bfloat16


def _rms(y):
    return y * lax.rsqrt(jnp.mean(y * y, axis=-1, keepdims=True) + EPS)


def _lane_low_half(shape):
    lane = lax.broadcasted_iota(jnp.int32, shape, len(shape) - 1)
    return (lane % LANES) < HEAD_DIM


def _rope_table_kernel(pos_ref, cos_ref, sin_ref):
    lane = lax.broadcasted_iota(jnp.int32, pos_ref.shape, 1)
    f = (lane % (HEAD_DIM // 2)).astype(F32)
    inv_freq = jnp.exp(-math.log(ROPE_THETA) * f * (2.0 / HEAD_DIM))
    ang = pos_ref[...] * inv_freq
    cos_ref[...] = jnp.cos(ang)
    sin_ref[...] = jnp.sin(ang)


def _rope_tables(positions):
    t = positions.size
    half = HEAD_DIM // 2
    rows = t * half // LANES
    pos = jnp.repeat(positions.reshape(t).astype(F32), half).reshape(rows, LANES)
    tr = 512
    cos_c, sin_c = pl.pallas_call(
        _rope_table_kernel,
        grid=(rows // tr,),
        in_specs=[pl.BlockSpec((tr, LANES), lambda i: (i, 0))],
        out_specs=[pl.BlockSpec((tr, LANES), lambda i: (i, 0))] * 2,
        out_shape=[jax.ShapeDtypeStruct((rows, LANES), F32)] * 2,
        name="rope_tables",
    )(pos)
    return cos_c.reshape(t, half), sin_c.reshape(t, half)


def _dft_matrix_kernel(c_ref, s_ref, ac_ref, as_ref, bc_ref, bs_ref, *, n):
    t = pl.program_id(0)
    grp = ac_ref.shape[0]

    @pl.when(t == 0)
    def _():
        r = lax.broadcasted_iota(jnp.int32, ac_ref.shape, 0)
        k = lax.broadcasted_iota(jnp.int32, ac_ref.shape, 1)
        a = (((r * k) & (grp - 1)).astype(F32)) * (2.0 * math.pi / grp)
        b = (((r * k) & (n - 1)).astype(F32)) * (2.0 * math.pi / n)
        ac_ref[...] = jnp.cos(a)
        as_ref[...] = jnp.sin(a)
        bc_ref[...] = jnp.cos(b)
        bs_ref[...] = jnp.sin(b)

    for s in range(pl.cdiv(DFT_ROWS, grp)):
        nrow = min(grp, DFT_ROWS - s * grp)
        j1 = t * (DFT_TILE // grp) + s
        ca = ac_ref[pl.ds(j1, 1), :]
        sa = as_ref[pl.ds(j1, 1), :]
        cb = bc_ref[:nrow, :]
        sb = bs_ref[:nrow, :]
        c_ref[s * grp:s * grp + nrow, :] = (ca * cb - sa * sb).astype(BF16)
        s_ref[s * grp:s * grp + nrow, :] = (-(sa * cb + ca * sb)).astype(BF16)


def _dft_matrices(n):
    grp = 64
    nt = n // (2 * DFT_TILE)
    assert n == grp * grp and DFT_TILE % grp == 0 and DFT_TILE < DFT_ROWS <= DFT_TILE + grp
    return pl.pallas_call(
        functools.partial(_dft_matrix_kernel, n=n),
        grid=(nt,),
        out_specs=[pl.BlockSpec((None, DFT_ROWS, n // 2), lambda i: (i, 0, 0))] * 2,
        out_shape=[jax.ShapeDtypeStruct((nt, DFT_ROWS, n // 2), BF16)] * 2,
        scratch_shapes=[pltpu.VMEM((grp, n // 2), F32)] * 4,
        compiler_params=pltpu.CompilerParams(dimension_semantics=("arbitrary",), vmem_limit_bytes=VMEM_LIMIT),
        name="dft_matrices",
    )()


def _fourier_fold_kernel(w_ref, ab_ref):
    shp = (FOURIER_WIDTH, FOURIER_WIDTH)
    r = lax.broadcasted_iota(jnp.int32, shp, 0)
    c = lax.broadcasted_iota(jnp.int32, shp, 1)
    same = (r // FOURIER_CH) == (c // FOURIER_CH)
    ang = ((((r % FOURIER_CH) * (c % FOURIER_CH)) % FOURIER_CH).astype(F32)) * (2.0 * math.pi / FOURIER_CH)
    cm = jnp.where(same, jnp.cos(ang), 0.0)
    sm = jnp.where(same, jnp.sin(ang), 0.0)
    w = w_ref[...]
    a = jnp.dot(cm, w, preferred_element_type=F32, precision=lax.Precision.HIGHEST)
    b = jnp.dot(sm, w, preferred_element_type=F32, precision=lax.Precision.HIGHEST)
    ab_ref[:, :FOURIER_WIDTH] = a.astype(BF16)
    ab_ref[:, FOURIER_WIDTH:] = b.astype(BF16)


def _fourier_fold(w_fourier):
    eye = jnp.eye(N_FOURIER_GROUPS, dtype=F32)
    w_bd = jnp.einsum("lgce,gh->lgche", w_fourier, eye).reshape(DEPTH, FOURIER_WIDTH, FOURIER_WIDTH)
    return pl.pallas_call(
        _fourier_fold_kernel,
        grid=(DEPTH,),
        in_specs=[pl.BlockSpec((None, FOURIER_WIDTH, FOURIER_WIDTH), lambda l: (l, 0, 0))],
        out_specs=pl.BlockSpec((None, FOURIER_WIDTH, 2 * FOURIER_WIDTH), lambda l: (l, 0, 0)),
        out_shape=jax.ShapeDtypeStruct((DEPTH, FOURIER_WIDTH, 2 * FOURIER_WIDTH), BF16),
        name="fourier_fold",
    )(w_bd)


def _mem_kv_kernel(mem_ref, g_ref, w_ref, kc_ref, vc_ref):
    h = (_rms(mem_ref[...]) * g_ref[...]).astype(BF16)
    mkv = jnp.dot(h, w_ref[...].astype(BF16), preferred_element_type=F32)
    low = _lane_low_half((mem_ref.shape[0], LANES))
    m = mem_ref.shape[0]
    for a in range(N_MEM_HEADS // 2):
        kp = mkv[:, a * LANES:(a + 1) * LANES]
        vp = mkv[:, MEM_WIDTH + a * LANES:MEM_WIDTH + (a + 1) * LANES]
        kc_ref[a, :m, :] = jnp.where(low, kp, 0.0).astype(BF16)
        kc_ref[a, m:, :] = jnp.where(low, 0.0, kp).astype(BF16)
        vc_ref[a, :m, :] = jnp.where(low, vp, 0.0).astype(BF16)
        vc_ref[a, m:, :] = jnp.where(low, 0.0, vp).astype(BF16)


def _mem_kv(mem, g_mem, w_mem_kv):
    b, m, _ = mem.shape
    npair = N_MEM_HEADS // 2
    out = jax.ShapeDtypeStruct((DEPTH, b, npair, 2 * m, LANES), BF16)
    spec = pl.BlockSpec((None, None, npair, 2 * m, LANES), lambda l, i: (l, i, 0, 0, 0))
    return pl.pallas_call(
        _mem_kv_kernel,
        grid=(DEPTH, b),
        in_specs=[
            pl.BlockSpec((None, m, D_MODEL), lambda l, i: (i, 0, 0)),
            pl.BlockSpec((None, 1, D_MODEL), lambda l, i: (l, 0, 0)),
            pl.BlockSpec((None, D_MODEL, 2 * MEM_WIDTH), lambda l, i: (l, 0, 0)),
        ],
        out_specs=[spec, spec],
        out_shape=[out, out],
        name="mem_kv",
    )(mem, g_mem.reshape(DEPTH, 1, D_MODEL), w_mem_kv)


def _in_proj_body(xv, g_ref, w_ref, ab_ref, cos_ref, sin_ref, q_ref, k_ref, v_ref, zm_ref, p_ref, qq_ref):
    h = (_rms(xv) * g_ref[...]).astype(BF16)
    z = jnp.dot(h, w_ref[...], preferred_element_type=F32)
    tm = z.shape[0]

    zf = z[:, :FOURIER_WIDTH].astype(BF16)
    pq = jnp.dot(zf, ab_ref[...], preferred_element_type=F32)
    p_ref[...] = pq[:, :FOURIER_WIDTH].astype(BF16)
    qq_ref[...] = pq[:, FOURIER_WIDTH:].astype(BF16)

    c32 = cos_ref[...]
    s32 = sin_ref[...]
    cos = jnp.concatenate([c32, c32, c32, c32], axis=1)
    sin = jnp.concatenate([-s32, s32, -s32, s32], axis=1)
    first = (lax.broadcasted_iota(jnp.int32, (tm, LANES), 1) % HEAD_DIM) < (HEAD_DIM // 2)

    def rope(t, c, s):
        rot = jnp.where(first, pltpu.roll(t, LANES - HEAD_DIM // 2, 1), pltpu.roll(t, HEAD_DIM // 2, 1))
        return t * c + rot * s

    q0 = FOURIER_WIDTH
    cq = cos * Q_SCALE
    sq = sin * Q_SCALE
    for c in range(ATTN_WIDTH // LANES):
        t = z[:, q0 + c * LANES:q0 + (c + 1) * LANES]
        q_ref[:, c * LANES:(c + 1) * LANES] = rope(t, cq, sq).astype(BF16)

    k0 = q0 + ATTN_WIDTH
    low = _lane_low_half((tm, LANES))
    kk = rope(z[:, k0:k0 + KV_WIDTH], cos, sin)
    vv = z[:, k0 + KV_WIDTH:k0 + 2 * KV_WIDTH]
    for src, dst in ((kk, k_ref), (vv, v_ref)):
        sw = pltpu.roll(src, HEAD_DIM, 1)
        dst[:, 0 * LANES:1 * LANES] = jnp.where(low, src, 0.0).astype(BF16)
        dst[:, 1 * LANES:2 * LANES] = jnp.where(low, 0.0, sw).astype(BF16)
        dst[:, 2 * LANES:3 * LANES] = jnp.where(low, sw, 0.0).astype(BF16)
        dst[:, 3 * LANES:4 * LANES] = jnp.where(low, 0.0, src).astype(BF16)

    m0 = k0 + 2 * KV_WIDTH
    zm_ref[...] = (z[:, m0:m0 + MEM_WIDTH] * Q_SCALE).astype(BF16)


def _in_proj_kernel(x_ref, *refs):
    _in_proj_body(x_ref[...], *refs)


def _in_proj_specs(l, b, s, tm):
    nt = s // tm
    tok = lambda w: pl.BlockSpec((None, tm, w), lambda i, j: (i, j, 0))
    seq_major = pl.BlockSpec((tm, FOURIER_WIDTH), lambda i, j: (j, i))
    rope_spec = pl.BlockSpec((tm, HEAD_DIM // 2), lambda i, j: (i * nt + j, 0))
    in_specs = [
        pl.BlockSpec((None, 1, D_MODEL), lambda i, j: (l, 0, 0)),
        pl.BlockSpec((D_MODEL, IN_WIDTH), lambda i, j: (0, 0), pipeline_mode=pl.Buffered(1)),
        pl.BlockSpec((None, FOURIER_WIDTH, 2 * FOURIER_WIDTH), lambda i, j: (l, 0, 0),
                     pipeline_mode=pl.Buffered(1)),
        rope_spec, rope_spec,
    ]
    out_specs = [tok(ATTN_WIDTH), tok(4 * LANES), tok(4 * LANES), tok(MEM_WIDTH), seq_major, seq_major]
    out_shape = [
        jax.ShapeDtypeStruct((b, s, ATTN_WIDTH), BF16),
        jax.ShapeDtypeStruct((b, s, 4 * LANES), BF16),
        jax.ShapeDtypeStruct((b, s, 4 * LANES), BF16),
        jax.ShapeDtypeStruct((b, s, MEM_WIDTH), BF16),
        jax.ShapeDtypeStruct((s, b * FOURIER_WIDTH), BF16),
        jax.ShapeDtypeStruct((s, b * FOURIER_WIDTH), BF16),
    ]
    return in_specs, out_specs, out_shape


def _in_proj(x, g_pre_mix, w_in_b, ab, cos_t, sin_t, *, l, tm):
    b, s, _ = x.shape
    in_specs, out_specs, out_shape = _in_proj_specs(l, b, s, tm)
    return pl.pallas_call(
        _in_proj_kernel,
        grid=(b, s // tm),
        in_specs=[pl.BlockSpec((None, tm, D_MODEL), lambda i, j: (i, j, 0))] + in_specs,
        out_specs=out_specs,
        out_shape=out_shape,
        compiler_params=pltpu.CompilerParams(
            dimension_semantics=("arbitrary", "arbitrary"), vmem_limit_bytes=VMEM_LIMIT),
        name="in_proj",
    )(x, g_pre_mix, w_in_b, ab, cos_t, sin_t)


def _softmax_parts(sc, sink):
    m = jnp.max(sc, axis=-1, keepdims=True)
    if sink is None:
        return jnp.exp2(sc - m).astype(BF16), None
    m = jnp.maximum(m, sink)
    return jnp.exp2(sc - m).astype(BF16), jnp.exp2(sink - m)


def _head_indicator(rows):
    low = _lane_low_half((rows, LANES))
    top = jnp.where(low, 1.0, 0.0).astype(BF16)
    bot = jnp.where(low, 0.0, 1.0).astype(BF16)
    return jnp.concatenate([top, bot], axis=0)


def _attn_kernel(sink_ref, q_ref, k_ref, v_ref, zm_ref, kmc_ref, vmc_ref, gg_ref,
                 ya_ref, ym_ref, *, seq, layer):
    ga = gg_ref[:, FOURIER_WIDTH:FOURIER_WIDTH + ATTN_WIDTH]
    gm = gg_ref[:, FOURIER_WIDTH + ATTN_WIDTH:]
    tq = q_ref.shape[0]
    nblk = tq // BLOCK
    band = 3 * BLOCK
    t = pl.program_id(1)
    rel = (lax.broadcasted_iota(jnp.int32, (BLOCK, band), 0)
           - lax.broadcasted_iota(jnp.int32, (BLOCK, band), 1))
    low = _lane_low_half((BLOCK, LANES))
    pairs_per_kv = N_Q_HEADS // N_KV_HEADS // 2
    ind_band = _head_indicator(band)

    for jj in range(nblk):
        blk = t * nblk + jj
        s0 = pl.multiple_of(jnp.clip((blk - 1) * BLOCK, 0, seq - band), BLOCK)
        d = blk * BLOCK - s0
        valid = jnp.abs(rel + d) <= WINDOW
        rows = pl.ds(jj * BLOCK, BLOCK)
        outs = []
        for g in range(N_KV_HEADS):
            kcat = jnp.concatenate(
                [k_ref[pl.ds(s0, band), (2 * g) * LANES:(2 * g + 1) * LANES],
                 k_ref[pl.ds(s0, band), (2 * g + 1) * LANES:(2 * g + 2) * LANES]], axis=0)
            vcat = jnp.concatenate(
                [v_ref[pl.ds(s0, band), (2 * g) * LANES:(2 * g + 1) * LANES],
                 v_ref[pl.ds(s0, band), (2 * g + 1) * LANES:(2 * g + 2) * LANES]], axis=0)
            qg = jnp.concatenate(
                [q_ref[rows, (pairs_per_kv * g + p) * LANES:(pairs_per_kv * g + p + 1) * LANES]
                 for p in range(pairs_per_kv)], axis=0)
            sc = lax.dot_general(qg, kcat, (((1,), (1,)), ((), ())), preferred_element_type=F32)
            probs, sink_terms = [], []
            for p in range(pairs_per_kv):
                row_p, row_s = [], []
                for e in range(2):
                    head = (N_Q_HEADS // N_KV_HEADS) * g + 2 * p + e
                    sb = sc[p * BLOCK:(p + 1) * BLOCK, e * band:(e + 1) * band]
                    sb = jnp.where(valid, sb, NEG_INF)
                    pe, st = _softmax_parts(sb, sink_ref[layer, head] * LOG2E)
                    row_p.append(pe)
                    row_s.append(st)
                probs.append(jnp.concatenate(row_p, axis=1))
                sink_terms.append(row_s)
            pm = jnp.concatenate(probs, axis=0)
            o = jnp.dot(pm, jnp.concatenate([vcat, ind_band], axis=1), preferred_element_type=F32)
            for p in range(pairs_per_kv):
                op = o[p * BLOCK:(p + 1) * BLOCK, :]
                den = op[:, LANES:] + jnp.where(low, sink_terms[p][0], sink_terms[p][1])
                outs.append(op[:, :LANES] / den)
        y = jnp.concatenate(outs, axis=1)
        ya_ref[rows, :] = (_rms(y) * ga).astype(BF16)

    nmem = kmc_ref.shape[1] // 2
    ind_mem = _head_indicator(nmem)
    outs = []
    for a in range(N_MEM_HEADS // 2):
        qa = zm_ref[:, a * LANES:(a + 1) * LANES]
        sc = lax.dot_general(qa, kmc_ref[a], (((1,), (1,)), ((), ())), preferred_element_type=F32)
        p0, _ = _softmax_parts(sc[:, :nmem], None)
        p1, _ = _softmax_parts(sc[:, nmem:], None)
        o = jnp.dot(jnp.concatenate([p0, p1], axis=1), jnp.concatenate([vmc_ref[a], ind_mem], axis=1),
                    preferred_element_type=F32)
        outs.append(o[:, :LANES] / o[:, LANES:])
    ym = jnp.concatenate(outs, axis=1)
    ym_ref[...] = (_rms(ym) * gm).astype(BF16)


def _attention(sink, q, k4, v4, zm, kmc, vmc, g_grp, *, l, tq):
    b, s, _ = q.shape
    nt = s // tq
    npair, m2, _ = kmc.shape[2:]
    tok = lambda w: pl.BlockSpec((None, tq, w), lambda i, j: (i, j, 0))
    per_batch = lambda w: pl.BlockSpec((None, s, w), lambda i, j: (i, 0, 0))
    memspec = pl.BlockSpec((None, None, npair, m2, LANES), lambda i, j: (l, i, 0, 0, 0))
    return pl.pallas_call(
        functools.partial(_attn_kernel, seq=s, layer=l),
        grid=(b, nt),
        in_specs=[
            pl.BlockSpec(memory_space=pltpu.SMEM),
            tok(ATTN_WIDTH), per_batch(4 * LANES), per_batch(4 * LANES), tok(MEM_WIDTH),
            memspec, memspec,
            pl.BlockSpec((None, 1, D_MODEL), lambda i, j: (l, 0, 0)),
        ],
        out_specs=[tok(ATTN_WIDTH), tok(MEM_WIDTH)],
        out_shape=[jax.ShapeDtypeStruct((b, s, ATTN_WIDTH), BF16),
                   jax.ShapeDtypeStruct((b, s, MEM_WIDTH), BF16)],
        compiler_params=pltpu.CompilerParams(
            dimension_semantics=("arbitrary", "arbitrary"), vmem_limit_bytes=VMEM_LIMIT),
        name="attention",
    )(sink, q, k4, v4, zm, kmc, vmc, g_grp)


def _exchange_matrix(size):
    ra = lax.broadcasted_iota(jnp.int32, (size, size), 0)
    rr = lax.broadcasted_iota(jnp.int32, (size, size), 1)
    return jnp.where(rr == size - ra, 1.0, 0.0).astype(BF16)


def _seq_dft_kernel(c_ref, s_ref, p_ref, q_ref, g_ref, y_ref, hi_ref, pp_ref, qm_ref, *, scale):
    i = pl.program_id(0)
    nt = hi_ref.shape[0]
    n = p_ref.shape[0]
    half = n // 2

    @pl.when(i == 0)
    def _():
        flip = _exchange_matrix(PAIR_TILE)
        first = lax.broadcasted_iota(jnp.int32, (PAIR_TILE, p_ref.shape[1]), 0) == 0
        for a in range(half // PAIR_TILE):
            lo = a * PAIR_TILE
            src = n - lo - PAIR_TILE
            for x_ref, dst, sgn in ((p_ref, pp_ref, 1.0), (q_ref, qm_ref, -1.0)):
                rev = jnp.dot(flip, x_ref[src:src + PAIR_TILE, :], preferred_element_type=F32)
                if a > 0:
                    edge = x_ref[src + PAIR_TILE:src + PAIR_TILE + 16, :][0:1].astype(F32)
                    rev = jnp.where(first, edge, rev)
                dst[lo:lo + PAIR_TILE, :] = (x_ref[lo:lo + PAIR_TILE, :].astype(F32) + sgn * rev).astype(BF16)

    def group_norm(y):
        nb = y.shape[1] // FOURIER_WIDTH
        return jnp.concatenate(
            [(_rms(y[:, b * FOURIER_WIDTH:(b + 1) * FOURIER_WIDTH]) * g_ref[:, :FOURIER_WIDTH]).astype(BF16)
             for b in range(nb)], axis=1)

    @pl.when(i < nt)
    def _():
        mid = p_ref[half:half + 16, :][0:1].astype(F32)
        odd = (lax.broadcasted_iota(jnp.int32, (DFT_ROWS, 1), 0) & 1) == 1
        a1 = jnp.dot(c_ref[...], pp_ref[...], preferred_element_type=F32)
        a1 = (a1 + jnp.where(odd, -mid, mid)) * scale
        a2 = jnp.dot(s_ref[...], qm_ref[...], preferred_element_type=F32) * scale
        y_ref[...] = group_norm((a1 + a2)[:DFT_TILE])
        hi = group_norm(a1 - a2)
        mirrored = jnp.dot(_exchange_matrix(DFT_TILE), hi[:DFT_TILE], preferred_element_type=F32).astype(BF16)
        first = lax.broadcasted_iota(jnp.int32, mirrored.shape, 0) == 0
        hi_ref[i] = jnp.where(first, hi[DFT_TILE:DFT_TILE + 1], mirrored)

    @pl.when(i >= nt)
    def _():
        y_ref[...] = hi_ref[2 * nt - 1 - i]


def _seq_dft(cmat, smat, p2, q2, g_grp, *, l):
    s, w = p2.shape
    nt = cmat.shape[0]
    scale = 1.0 / math.sqrt(s * FOURIER_CH)
    resident = pl.BlockSpec((s, w), lambda i: (0, 0), pipeline_mode=pl.Buffered(1))
    half = pl.BlockSpec((None, DFT_ROWS, s // 2), lambda i: (jnp.minimum(i, nt - 1), 0, 0))
    return pl.pallas_call(
        functools.partial(_seq_dft_kernel, scale=scale),
        grid=(2 * nt,),
        in_specs=[half, half, resident, resident, pl.BlockSpec((None, 1, D_MODEL), lambda i: (l, 0, 0))],
        out_specs=pl.BlockSpec((DFT_TILE, w), lambda i: (i, 0)),
        out_shape=jax.ShapeDtypeStruct((s, w), BF16),
        scratch_shapes=[pltpu.VMEM((nt, DFT_TILE, w), BF16),
                        pltpu.VMEM((s // 2, w), BF16), pltpu.VMEM((s // 2, w), BF16)],
        compiler_params=pltpu.CompilerParams(
            dimension_semantics=("arbitrary",), vmem_limit_bytes=VMEM_LIMIT),
        name="seq_dft",
    )(cmat, smat, p2, q2, g_grp)


def _out_ffn_kernel(yf_ref, ya_ref, ym_ref, x_ref, wo_ref, gpm_ref, gpf_ref, w1_ref, w2_ref, gpo_ref,
                    *rest, fuse_next, n_cast):
    n_next_in = 5 if fuse_next else 0
    n_next_out = 6 if fuse_next else 0
    next_in = rest[:n_next_in]
    cast_src = rest[n_next_in:n_next_in + n_cast]
    outs = rest[n_next_in + n_cast:-3]
    xo_ref, next_out, cast_dst = outs[0], outs[1:1 + n_next_out], outs[1 + n_next_out:]
    x1_ref, h_ref, acc_ref = rest[-3:]
    for src, dst in zip(cast_src, cast_dst):
        dst[...] = src[...].astype(BF16)
    ycat = jnp.concatenate([yf_ref[...], ya_ref[...], ym_ref[...]], axis=1)
    y = jnp.dot(ycat, wo_ref[...], preferred_element_type=F32)
    x1 = x_ref[...] + _rms(y) * gpm_ref[...]
    x1_ref[...] = x1
    h_ref[...] = (_rms(x1) * gpf_ref[...]).astype(BF16)
    for c in range(N_FF_CHUNKS):
        cols = slice(c * FF_CHUNK, (c + 1) * FF_CHUNK)
        gate = jnp.dot(h_ref[...], w1_ref[:, cols], preferred_element_type=F32)
        up = jnp.dot(h_ref[...], w1_ref[:, D_FF + c * FF_CHUNK:D_FF + (c + 1) * FF_CHUNK],
                     preferred_element_type=F32)
        f = (gate * jax.nn.sigmoid(gate) * up).astype(BF16)
        part = jnp.dot(f, w2_ref[cols, :], preferred_element_type=F32)
        if c == 0:
            acc_ref[...] = part
        else:
            acc_ref[...] += part
    x2 = x1_ref[...] + _rms(acc_ref[...]) * gpo_ref[...]
    xo_ref[...] = x2
    if fuse_next:
        _in_proj_body(x2, *next_in, *next_out)


def _out_ffn(yf, ya, ym, x, w_out_l, g_post_mix, g_pre_ffn, w1_l, w2_l, g_post_ffn, next_args, casts, *, l, tm):
    b, s, _ = x.shape
    nt = s // tm
    steps = b * nt
    tok = lambda w: pl.BlockSpec((None, tm, w), lambda i, j: (i, j, 0))
    vec = pl.BlockSpec((None, 1, D_MODEL), lambda i, j: (l, 0, 0))
    one = pl.Buffered(1)
    flat = lambda r, c: pl.BlockSpec((r, c), lambda i, j: (0, 0), pipeline_mode=one)
    in_specs = [
        pl.BlockSpec((tm, FOURIER_WIDTH), lambda i, j: (j, i)),
        tok(ATTN_WIDTH), tok(MEM_WIDTH), tok(D_MODEL),
        flat(D_MODEL, D_MODEL), vec, vec, flat(D_MODEL, 2 * D_FF), flat(D_FF, D_MODEL), vec,
    ]
    out_specs = [tok(D_MODEL)]
    out_shape = [jax.ShapeDtypeStruct((b, s, D_MODEL), F32)]
    fuse_next = next_args is not None
    if fuse_next:
        nin, nout, nshape = _in_proj_specs(l + 1, b, s, tm)
        in_specs += nin
        out_specs += nout
        out_shape += nshape
    for w, layer, rows in casts:
        n_rows, n_cols = w.shape[1:]
        last = n_rows // rows - 1
        assert n_rows % rows == 0 and rows % 16 == 0 and last < steps
        in_specs.append(pl.BlockSpec(
            (None, rows, n_cols), lambda i, j, layer=layer, last=last: (layer, jnp.minimum(i * nt + j, last), 0)))
        out_specs.append(pl.BlockSpec(
            (rows, n_cols), lambda i, j, last=last: (jnp.minimum(i * nt + j, last), 0)))
        out_shape.append(jax.ShapeDtypeStruct((n_rows, n_cols), BF16))
    return pl.pallas_call(
        functools.partial(_out_ffn_kernel, fuse_next=fuse_next, n_cast=len(casts)),
        grid=(b, nt),
        in_specs=in_specs,
        out_specs=out_specs,
        out_shape=out_shape,
        scratch_shapes=[pltpu.VMEM((tm, D_MODEL), F32), pltpu.VMEM((tm, D_MODEL), BF16),
                        pltpu.VMEM((tm, D_MODEL), F32)],
        compiler_params=pltpu.CompilerParams(
            dimension_semantics=("arbitrary", "arbitrary"), vmem_limit_bytes=VMEM_LIMIT),
        name="out_ffn",
    )(yf, ya, ym, x, w_out_l, g_post_mix, g_pre_ffn, w1_l, w2_l, g_post_ffn,
      *(next_args or ()), *(w for w, _, _ in casts))


def kernel(x, mem, positions, g_pre_mix, w_in, w_fourier, sink, g_mem, w_mem_kv, g_grp,
           w_out, g_post_mix, g_pre_ffn, w_ffn_in, w_ffn_out, g_post_ffn):
    b, s, _ = x.shape
    assert s % 512 == 0 and s >= 3 * BLOCK and s == 64 * 64

    w_in_b = {0: w_in[0].astype(BF16), 1: w_in[1].astype(BF16)}
    w_out_b = {0: w_out[0].astype(BF16)}
    w1 = {0: w_ffn_in[0].astype(BF16)}
    w2 = {0: w_ffn_out[0].astype(BF16)}

    cos_t, sin_t = _rope_tables(positions)
    cmat, smat = _dft_matrices(s)
    ab = _fourier_fold(w_fourier)
    kmc, vmc = _mem_kv(mem, g_mem, w_mem_kv)

    vec3 = lambda g: g.reshape(DEPTH, 1, g.shape[-1])
    g_pre_mix, g_grp, g_post_mix, g_pre_ffn, g_post_ffn = map(
        vec3, (g_pre_mix, g_grp, g_post_mix, g_pre_ffn, g_post_ffn))

    in_args = lambda l: (g_pre_mix, w_in_b[l], ab, cos_t, sin_t)
    mix = _in_proj(x, *in_args(0), l=0, tm=1024)
    for l in range(DEPTH):
        q, k4, v4, zm, p2, q2 = mix
        ya, ym = _attention(sink, q, k4, v4, zm, kmc, vmc, g_grp, l=l, tq=2048)
        yf = _seq_dft(cmat, smat, p2, q2, g_grp, l=l)
        casts = []
        if l + 1 < DEPTH:
            casts += [(w_out, l + 1, 32), (w_ffn_in, l + 1, 32), (w_ffn_out, l + 1, 176)]
        if l + 2 < DEPTH:
            casts += [(w_in, l + 2, 32)]
        res = _out_ffn(yf, ya, ym, x, w_out_b[l], g_post_mix, g_pre_ffn, w1[l], w2[l], g_post_ffn,
                       in_args(l + 1) if l + 1 < DEPTH else None, casts, l=l, tm=512)
        x = res[0]
        if l + 1 < DEPTH:
            mix = res[1:7]
            w_out_b[l + 1], w1[l + 1], w2[l + 1] = res[7:10]
        if l + 2 < DEPTH:
            w_in_b[l + 2] = res[10]
    return x
```

```python
import functools
import math

import jax
import jax.numpy as jnp
from jax import lax
from jax.experimental import pallas as pl
from jax.experimental.pallas import tpu as pltpu

D_MODEL = 1024
DEPTH = 4
HEAD_DIM = 64
FOURIER_WIDTH = 256
FOURIER_CH = 64
N_FOURIER_GROUPS = 4
ATTN_WIDTH = 512
N_Q_HEADS = 8
N_KV_HEADS = 2
KV_WIDTH = 128
MEM_WIDTH = 256
N_MEM_HEADS = 4
IN_WIDTH = 1280
WINDOW = 128
BLOCK = 128
ROPE_THETA = 10000.0
D_FF = 2816
EPS = 1e-6
NEG_INF = -1e30

LANES = 128
VMEM_LIMIT = 56 * 1024 * 1024
FF_CHUNK = 256
N_FF_CHUNKS = D_FF // FF_CHUNK
LOG2E = math.log2(math.e)
Q_SCALE = HEAD_DIM ** -0.5 * LOG2E
DFT_TILE = 512
DFT_ROWS = DFT_TILE + 16
PAIR_TILE = 256

F32 = jnp.float32
BF16 = jnp.bfloat16


def _rms(y):
    return y * lax.rsqrt(jnp.mean(y * y, axis=-1, keepdims=True) + EPS)


def _lane_low_half(shape):
    lane = lax.broadcasted_iota(jnp.int32, shape, len(shape) - 1)
    return (lane % LANES) < HEAD_DIM


def _rope_table_kernel(pos_ref, cos_ref, sin_ref):
    lane = lax.broadcasted_iota(jnp.int32, pos_ref.shape, 1)
    f = (lane % (HEAD_DIM // 2)).astype(F32)
    inv_freq = jnp.exp(-math.log(ROPE_THETA) * f * (2.0 / HEAD_DIM))
    ang = pos_ref[...] * inv_freq
    cos_ref[...] = jnp.cos(ang)
    sin_ref[...] = jnp.sin(ang)


def _rope_tables(positions):
    t = positions.size
    half = HEAD_DIM // 2
    rows = t * half // LANES
    pos = jnp.repeat(positions.reshape(t).astype(F32), half).reshape(rows, LANES)
    tr = 512
    cos_c, sin_c = pl.pallas_call(
        _rope_table_kernel,
        grid=(rows // tr,),
        in_specs=[pl.BlockSpec((tr, LANES), lambda i: (i, 0))],
        out_specs=[pl.BlockSpec((tr, LANES), lambda i: (i, 0))] * 2,
        out_shape=[jax.ShapeDtypeStruct((rows, LANES), F32)] * 2,
        name="rope_tables",
    )(pos)
    return cos_c.reshape(t, half), sin_c.reshape(t, half)


def _dft_matrix_kernel(c_ref, s_ref, ac_ref, as_ref, bc_ref, bs_ref, *, n):
    t = pl.program_id(0)
    grp = ac_ref.shape[0]

    @pl.when(t == 0)
    def _():
        r = lax.broadcasted_iota(jnp.int32, ac_ref.shape, 0)
        k = lax.broadcasted_iota(jnp.int32, ac_ref.shape, 1)
        a = (((r * k) & (grp - 1)).astype(F32)) * (2.0 * math.pi / grp)
        b = (((r * k) & (n - 1)).astype(F32)) * (2.0 * math.pi / n)
        ac_ref[...] = jnp.cos(a)
        as_ref[...] = jnp.sin(a)
        bc_ref[...] = jnp.cos(b)
        bs_ref[...] = jnp.sin(b)

    for s in range(pl.cdiv(DFT_ROWS, grp)):
        nrow = min(grp, DFT_ROWS - s * grp)
        j1 = t * (DFT_TILE // grp) + s
        ca = ac_ref[pl.ds(j1, 1), :]
        sa = as_ref[pl.ds(j1, 1), :]
        cb = bc_ref[:nrow, :]
        sb = bs_ref[:nrow, :]
        c_ref[s * grp:s * grp + nrow, :] = (ca * cb - sa * sb).astype(BF16)
        s_ref[s * grp:s * grp + nrow, :] = (-(sa * cb + ca * sb)).astype(BF16)


def _dft_matrices(n):
    grp = 64
    nt = n // (2 * DFT_TILE)
    assert n == grp * grp and DFT_TILE % grp == 0 and DFT_TILE < DFT_ROWS <= DFT_TILE + grp
    return pl.pallas_call(
        functools.partial(_dft_matrix_kernel, n=n),
        grid=(nt,),
        out_specs=[pl.BlockSpec((None, DFT_ROWS, n // 2), lambda i: (i, 0, 0))] * 2,
        out_shape=[jax.ShapeDtypeStruct((nt, DFT_ROWS, n // 2), BF16)] * 2,
        scratch_shapes=[pltpu.VMEM((grp, n // 2), F32)] * 4,
        compiler_params=pltpu.CompilerParams(dimension_semantics=("arbitrary",), vmem_limit_bytes=VMEM_LIMIT),
        name="dft_matrices",
    )()


def _fourier_fold_kernel(w_ref, ab_ref):
    shp = (FOURIER_WIDTH, FOURIER_WIDTH)
    r = lax.broadcasted_iota(jnp.int32, shp, 0)
    c = lax.broadcasted_iota(jnp.int32, shp, 1)
    same = (r // FOURIER_CH) == (c // FOURIER_CH)
    ang = ((((r % FOURIER_CH) * (c % FOURIER_CH)) % FOURIER_CH).astype(F32)) * (2.0 * math.pi / FOURIER_CH)
    cm = jnp.where(same, jnp.cos(ang), 0.0)
    sm = jnp.where(same, jnp.sin(ang), 0.0)
    w = w_ref[...]
    a = jnp.dot(cm, w, preferred_element_type=F32, precision=lax.Precision.HIGHEST)
    b = jnp.dot(sm, w, preferred_element_type=F32, precision=lax.Precision.HIGHEST)
    ab_ref[:, :FOURIER_WIDTH] = a.astype(BF16)
    ab_ref[:, FOURIER_WIDTH:] = b.astype(BF16)


def _fourier_fold(w_fourier):
    eye = jnp.eye(N_FOURIER_GROUPS, dtype=F32)
    w_bd = jnp.einsum("lgce,gh->lgche", w_fourier, eye).reshape(DEPTH, FOURIER_WIDTH, FOURIER_WIDTH)
    return pl.pallas_call(
        _fourier_fold_kernel,
        grid=(DEPTH,),
        in_specs=[pl.BlockSpec((None, FOURIER_WIDTH, FOURIER_WIDTH), lambda l: (l, 0, 0))],
        out_specs=pl.BlockSpec((None, FOURIER_WIDTH, 2 * FOURIER_WIDTH), lambda l: (l, 0, 0)),
        out_shape=jax.ShapeDtypeStruct((DEPTH, FOURIER_WIDTH, 2 * FOURIER_WIDTH), BF16),
        name="fourier_fold",
    )(w_bd)


def _mem_kv_kernel(mem_ref, g_ref, w_ref, kc_ref, vc_ref):
    nb, _, two_m, _ = kc_ref.shape
    m = two_m // 2
    h = (_rms(mem_ref[...]) * g_ref[...]).astype(BF16)
    mkv = jnp.dot(h, w_ref[...].astype(BF16), preferred_element_type=F32)
    low = _lane_low_half((m, LANES))
    for i in range(nb):
        rows = slice(i * m, (i + 1) * m)
        for a in range(N_MEM_HEADS // 2):
            kp = mkv[rows, a * LANES:(a + 1) * LANES]
            vp = mkv[rows, MEM_WIDTH + a * LANES:MEM_WIDTH + (a + 1) * LANES]
            kc_ref[i, a, :m, :] = jnp.where(low, kp, 0.0).astype(BF16)
            kc_ref[i, a, m:, :] = jnp.where(low, 0.0, kp).astype(BF16)
            vc_ref[i, a, :m, :] = jnp.where(low, vp, 0.0).astype(BF16)
            vc_ref[i, a, m:, :] = jnp.where(low, 0.0, vp).astype(BF16)


def _mem_kv(mem, g_mem, w_mem_kv):
    b, m, _ = mem.shape
    npair = N_MEM_HEADS // 2
    out = jax.ShapeDtypeStruct((DEPTH, b, npair, 2 * m, LANES), BF16)
    spec = pl.BlockSpec((None, b, npair, 2 * m, LANES), lambda l: (l, 0, 0, 0, 0))
    return pl.pallas_call(
        _mem_kv_kernel,
        grid=(DEPTH,),
        in_specs=[
            pl.BlockSpec((b * m, D_MODEL), lambda l: (0, 0)),
            pl.BlockSpec((None, 1, D_MODEL), lambda l: (l, 0, 0)),
            pl.BlockSpec((None, D_MODEL, 2 * MEM_WIDTH), lambda l: (l, 0, 0)),
        ],
        out_specs=[spec, spec],
        out_shape=[out, out],
        name="mem_kv",
    )(mem.reshape(b * m, D_MODEL), g_mem.reshape(DEPTH, 1, D_MODEL), w_mem_kv)


def _in_proj_body(xv, g_ref, w_ref, ab_ref, cos_ref, sin_ref, q_ref, k_ref, v_ref, zm_ref, p_ref, qq_ref):
    h = (_rms(xv) * g_ref[...]).astype(BF16)
    z = jnp.dot(h, w_ref[...], preferred_element_type=F32)
    tm = z.shape[0]

    zf = z[:, :FOURIER_WIDTH].astype(BF16)
    pq = jnp.dot(zf, ab_ref[...], preferred_element_type=F32)
    p_ref[...] = pq[:, :FOURIER_WIDTH].astype(BF16)
    qq_ref[...] = pq[:, FOURIER_WIDTH:].astype(BF16)

    c32 = cos_ref[...]
    s32 = sin_ref[...]
    cos = jnp.concatenate([c32, c32, c32, c32], axis=1)
    sin = jnp.concatenate([-s32, s32, -s32, s32], axis=1)
    first = (lax.broadcasted_iota(jnp.int32, (tm, LANES), 1) % HEAD_DIM) < (HEAD_DIM // 2)

    def rope(t, c, s):
        rot = jnp.where(first, pltpu.roll(t, LANES - HEAD_DIM // 2, 1), pltpu.roll(t, HEAD_DIM // 2, 1))
        return t * c + rot * s

    q0 = FOURIER_WIDTH
    cq = cos * Q_SCALE
    sq = sin * Q_SCALE
    for c in range(ATTN_WIDTH // LANES):
        t = z[:, q0 + c * LANES:q0 + (c + 1) * LANES]
        q_ref[:, c * LANES:(c + 1) * LANES] = rope(t, cq, sq).astype(BF16)

    k0 = q0 + ATTN_WIDTH
    low = _lane_low_half((tm, LANES))
    kk = rope(z[:, k0:k0 + KV_WIDTH], cos, sin)
    vv = z[:, k0 + KV_WIDTH:k0 + 2 * KV_WIDTH]
    for src, dst in ((kk, k_ref), (vv, v_ref)):
        sw = pltpu.roll(src, HEAD_DIM, 1)
        dst[:, 0 * LANES:1 * LANES] = jnp.where(low, src, 0.0).astype(BF16)
        dst[:, 1 * LANES:2 * LANES] = jnp.where(low, 0.0, sw).astype(BF16)
        dst[:, 2 * LANES:3 * LANES] = jnp.where(low, sw, 0.0).astype(BF16)
        dst[:, 3 * LANES:4 * LANES] = jnp.where(low, 0.0, src).astype(BF16)

    m0 = k0 + 2 * KV_WIDTH
    zm_ref[...] = (z[:, m0:m0 + MEM_WIDTH] * Q_SCALE).astype(BF16)


def _in_proj_kernel(x_ref, *refs):
    _in_proj_body(x_ref[...], *refs)


def _in_proj_specs(l, b, s, tm):
    nt = s // tm
    tok = lambda w: pl.BlockSpec((None, tm, w), lambda i, j: (i, j, 0))
    seq_major = pl.BlockSpec((tm, FOURIER_WIDTH), lambda i, j: (j, i))
    rope_spec = pl.BlockSpec((tm, HEAD_DIM // 2), lambda i, j: (i * nt + j, 0))
    in_specs = [
        pl.BlockSpec((None, 1, D_MODEL), lambda i, j: (l, 0, 0)),
        pl.BlockSpec((D_MODEL, IN_WIDTH), lambda i, j: (0, 0), pipeline_mode=pl.Buffered(1)),
        pl.BlockSpec((None, FOURIER_WIDTH, 2 * FOURIER_WIDTH), lambda i, j: (l, 0, 0),
                     pipeline_mode=pl.Buffered(1)),
        rope_spec, rope_spec,
    ]
    out_specs = [tok(ATTN_WIDTH), tok(4 * LANES), tok(4 * LANES), tok(MEM_WIDTH), seq_major, seq_major]
    out_shape = [
        jax.ShapeDtypeStruct((b, s, ATTN_WIDTH), BF16),
        jax.ShapeDtypeStruct((b, s, 4 * LANES), BF16),
        jax.ShapeDtypeStruct((b, s, 4 * LANES), BF16),
        jax.ShapeDtypeStruct((b, s, MEM_WIDTH), BF16),
        jax.ShapeDtypeStruct((s, b * FOURIER_WIDTH), BF16),
        jax.ShapeDtypeStruct((s, b * FOURIER_WIDTH), BF16),
    ]
    return in_specs, out_specs, out_shape


def _in_proj(x, g_pre_mix, w_in_b, ab, cos_t, sin_t, *, l, tm):
    b, s, _ = x.shape
    in_specs, out_specs, out_shape = _in_proj_specs(l, b, s, tm)
    return pl.pallas_call(
        _in_proj_kernel,
        grid=(b, s // tm),
        in_specs=[pl.BlockSpec((None, tm, D_MODEL), lambda i, j: (i, j, 0))] + in_specs,
        out_specs=out_specs,
        out_shape=out_shape,
        compiler_params=pltpu.CompilerParams(
            dimension_semantics=("arbitrary", "arbitrary"), vmem_limit_bytes=VMEM_LIMIT),
        name="in_proj",
    )(x, g_pre_mix, w_in_b, ab, cos_t, sin_t)


def _softmax_parts(sc, sink):
    m = jnp.max(sc, axis=-1, keepdims=True)
    if sink is None:
        return jnp.exp2(sc - m).astype(BF16), None
    m = jnp.maximum(m, sink)
    return jnp.exp2(sc - m).astype(BF16), jnp.exp2(sink - m)


def _head_indicator(rows):
    low = _lane_low_half((rows, LANES))
    top = jnp.where(low, 1.0, 0.0).astype(BF16)
    bot = jnp.where(low, 0.0, 1.0).astype(BF16)
    return jnp.concatenate([top, bot], axis=0)


def _attn_kernel(sink_ref, q_ref, k_ref, v_ref, zm_ref, kmc_ref, vmc_ref, gg_ref,
                 ya_ref, ym_ref, *, seq, layer):
    ga = gg_ref[:, FOURIER_WIDTH:FOURIER_WIDTH + ATTN_WIDTH]
    gm = gg_ref[:, FOURIER_WIDTH + ATTN_WIDTH:]
    tq = q_ref.shape[0]
    nblk = tq // BLOCK
    band = 3 * BLOCK
    t = pl.program_id(1)
    rel = (lax.broadcasted_iota(jnp.int32, (BLOCK, band), 0)
           - lax.broadcasted_iota(jnp.int32, (BLOCK, band), 1))
    low = _lane_low_half((BLOCK, LANES))
    pairs_per_kv = N_Q_HEADS // N_KV_HEADS // 2
    ind_band = _head_indicator(band)

    for jj in range(nblk):
        blk = t * nblk + jj
        s0 = pl.multiple_of(jnp.clip((blk - 1) * BLOCK, 0, seq - band), BLOCK)
        d = blk * BLOCK - s0
        valid = jnp.abs(rel + d) <= WINDOW
        rows = pl.ds(jj * BLOCK, BLOCK)
        outs = []
        for g in range(N_KV_HEADS):
            kcat = jnp.concatenate(
                [k_ref[pl.ds(s0, band), (2 * g) * LANES:(2 * g + 1) * LANES],
                 k_ref[pl.ds(s0, band), (2 * g + 1) * LANES:(2 * g + 2) * LANES]], axis=0)
            vcat = jnp.concatenate(
                [v_ref[pl.ds(s0, band), (2 * g) * LANES:(2 * g + 1) * LANES],
                 v_ref[pl.ds(s0, band), (2 * g + 1) * LANES:(2 * g + 2) * LANES]], axis=0)
            qg = jnp.concatenate(
                [q_ref[rows, (pairs_per_kv * g + p) * LANES:(pairs_per_kv * g + p + 1) * LANES]
                 for p in range(pairs_per_kv)], axis=0)
            sc = lax.dot_general(qg, kcat, (((1,), (1,)), ((), ())), preferred_element_type=F32)
            probs, sink_terms = [], []
            for p in range(pairs_per_kv):
                row_p, row_s = [], []
                for e in range(2):
                    head = (N_Q_HEADS // N_KV_HEADS) * g + 2 * p + e
                    sb = sc[p * BLOCK:(p + 1) * BLOCK, e * band:(e + 1) * band]
                    sb = jnp.where(valid, sb, NEG_INF)
                    pe, st = _softmax_parts(sb, sink_ref[layer, head] * LOG2E)
                    row_p.append(pe)
                    row_s.append(st)
                probs.append(jnp.concatenate(row_p, axis=1))
                sink_terms.append(row_s)
            pm = jnp.concatenate(probs, axis=0)
            o = jnp.dot(pm, jnp.concatenate([vcat, ind_band], axis=1), preferred_element_type=F32)
            for p in range(pairs_per_kv):
                op = o[p * BLOCK:(p + 1) * BLOCK, :]
                den = op[:, LANES:] + jnp.where(low, sink_terms[p][0], sink_terms[p][1])
                outs.append(op[:, :LANES] / den)
        y = jnp.concatenate(outs, axis=1)
        ya_ref[rows, :] = (_rms(y) * ga).astype(BF16)

    nmem = kmc_ref.shape[1] // 2
    ind_mem = _head_indicator(nmem)
    outs = []
    for a in range(N_MEM_HEADS // 2):
        qa = zm_ref[:, a * LANES:(a + 1) * LANES]
        sc = lax.dot_general(qa, kmc_ref[a], (((1,), (1,)), ((), ())), preferred_element_type=F32)
        p0, _ = _softmax_parts(sc[:, :nmem], None)
        p1, _ = _softmax_parts(sc[:, nmem:], None)
        o = jnp.dot(jnp.concatenate([p0, p1], axis=1), jnp.concatenate([vmc_ref[a], ind_mem], axis=1),
                    preferred_element_type=F32)
        outs.append(o[:, :LANES] / o[:, LANES:])
    ym = jnp.concatenate(outs, axis=1)
    ym_ref[...] = (_rms(ym) * gm).astype(BF16)


def _attention(sink, q, k4, v4, zm, kmc, vmc, g_grp, *, l, tq):
    b, s, _ = q.shape
    nt = s // tq
    npair, m2, _ = kmc.shape[2:]
    tok = lambda w: pl.BlockSpec((None, tq, w), lambda i, j: (i, j, 0))
    per_batch = lambda w: pl.BlockSpec((None, s, w), lambda i, j: (i, 0, 0))
    memspec = pl.BlockSpec((None, None, npair, m2, LANES), lambda i, j: (l, i, 0, 0, 0))
    return pl.pallas_call(
        functools.partial(_attn_kernel, seq=s, layer=l),
        grid=(b, nt),
        in_specs=[
            pl.BlockSpec(memory_space=pltpu.SMEM),
            tok(ATTN_WIDTH), per_batch(4 * LANES), per_batch(4 * LANES), tok(MEM_WIDTH),
            memspec, memspec,
            pl.BlockSpec((None, 1, D_MODEL), lambda i, j: (l, 0, 0)),
        ],
        out_specs=[tok(ATTN_WIDTH), tok(MEM_WIDTH)],
        out_shape=[jax.ShapeDtypeStruct((b, s, ATTN_WIDTH), BF16),
                   jax.ShapeDtypeStruct((b, s, MEM_WIDTH), BF16)],
        compiler_params=pltpu.CompilerParams(
            dimension_semantics=("arbitrary", "arbitrary"), vmem_limit_bytes=VMEM_LIMIT),
        name="attention",
    )(sink, q, k4, v4, zm, kmc, vmc, g_grp)


def _exchange_matrix(size):
    ra = lax.broadcasted_iota(jnp.int32, (size, size), 0)
    rr = lax.broadcasted_iota(jnp.int32, (size, size), 1)
    return jnp.where(rr == size - ra, 1.0, 0.0).astype(BF16)


def _seq_dft_kernel(c_ref, s_ref, p_ref, q_ref, g_ref, y_ref, pp_ref, qm_ref, *, scale):
    n = p_ref.shape[0]
    half = n // 2

    @pl.when(pl.program_id(0) == 0)
    def _():
        flip = _exchange_matrix(PAIR_TILE)
        first = lax.broadcasted_iota(jnp.int32, (PAIR_TILE, p_ref.shape[1]), 0) == 0
        for a in range(half // PAIR_TILE):
            lo = a * PAIR_TILE
            src = n - lo - PAIR_TILE
            for x_ref, dst, sgn in ((p_ref, pp_ref, 1.0), (q_ref, qm_ref, -1.0)):
                rev = jnp.dot(flip, x_ref[src:src + PAIR_TILE, :], preferred_element_type=F32)
                if a > 0:
                    edge = x_ref[src + PAIR_TILE:src + PAIR_TILE + 16, :][0:1].astype(F32)
                    rev = jnp.where(first, edge, rev)
                dst[lo:lo + PAIR_TILE, :] = (x_ref[lo:lo + PAIR_TILE, :].astype(F32) + sgn * rev).astype(BF16)

    def group_norm(y):
        nb = y.shape[1] // FOURIER_WIDTH
        return jnp.concatenate(
            [(_rms(y[:, b * FOURIER_WIDTH:(b + 1) * FOURIER_WIDTH]) * g_ref[:, :FOURIER_WIDTH]).astype(BF16)
             for b in range(nb)], axis=1)

    mid = p_ref[half:half + 16, :][0:1].astype(F32)
    odd = (lax.broadcasted_iota(jnp.int32, (DFT_ROWS, 1), 0) & 1) == 1
    a1 = jnp.dot(c_ref[...], pp_ref[...], preferred_element_type=F32)
    a1 = (a1 + jnp.where(odd, -mid, mid)) * scale
    a2 = jnp.dot(s_ref[...], qm_ref[...], preferred_element_type=F32) * scale
    y_ref[0] = group_norm((a1 + a2)[:DFT_TILE])
    hi = group_norm(a1 - a2)
    mirrored = jnp.dot(_exchange_matrix(DFT_TILE), hi[:DFT_TILE], preferred_element_type=F32).astype(BF16)
    first = lax.broadcasted_iota(jnp.int32, mirrored.shape, 0) == 0
    y_ref[1] = jnp.where(first, hi[DFT_TILE:DFT_TILE + 1], mirrored)


def _seq_dft(cmat, smat, p2, q2, g_grp, *, l):
    s, w = p2.shape
    nt = cmat.shape[0]
    scale = 1.0 / math.sqrt(s * FOURIER_CH)
    resident = pl.BlockSpec((s, w), lambda i: (0, 0), pipeline_mode=pl.Buffered(1))
    half = pl.BlockSpec((None, DFT_ROWS, s // 2), lambda i: (i, 0, 0))
    return pl.pallas_call(
        functools.partial(_seq_dft_kernel, scale=scale),
        grid=(nt,),
        in_specs=[half, half, resident, resident, pl.BlockSpec((None, 1, D_MODEL), lambda i: (l, 0, 0))],
        out_specs=pl.BlockSpec((2, DFT_TILE, w), lambda i: (0, i, 0)),
        out_shape=jax.ShapeDtypeStruct((2, s // 2, w), BF16),
        scratch_shapes=[pltpu.VMEM((s // 2, w), BF16), pltpu.VMEM((s // 2, w), BF16)],
        compiler_params=pltpu.CompilerParams(
            dimension_semantics=("arbitrary",), vmem_limit_bytes=VMEM_LIMIT),
        name="seq_dft",
    )(cmat, smat, p2, q2, g_grp)


def _out_ffn_kernel(yf_ref, ya_ref, ym_ref, x_ref, wo_ref, gpm_ref, gpf_ref, w1_ref, w2_ref, gpo_ref,
                    *rest, fuse_next, n_cast):
    n_next_in = 5 if fuse_next else 0
    n_next_out = 6 if fuse_next else 0
    next_in = rest[:n_next_in]
    cast_src = rest[n_next_in:n_next_in + n_cast]
    outs = rest[n_next_in + n_cast:-3]
    xo_ref, next_out, cast_dst = outs[0], outs[1:1 + n_next_out], outs[1 + n_next_out:]
    x1_ref, h_ref, acc_ref = rest[-3:]
    for src, dst in zip(cast_src, cast_dst):
        dst[...] = src[...].astype(BF16)
    ycat = jnp.concatenate([yf_ref[...], ya_ref[...], ym_ref[...]], axis=1)
    y = jnp.dot(ycat, wo_ref[...], preferred_element_type=F32)
    x1 = x_ref[...] + _rms(y) * gpm_ref[...]
    x1_ref[...] = x1
    h_ref[...] = (_rms(x1) * gpf_ref[...]).astype(BF16)
    for c in range(N_FF_CHUNKS):
        cols = slice(c * FF_CHUNK, (c + 1) * FF_CHUNK)
        gate = jnp.dot(h_ref[...], w1_ref[:, cols], preferred_element_type=F32)
        up = jnp.dot(h_ref[...], w1_ref[:, D_FF + c * FF_CHUNK:D_FF + (c + 1) * FF_CHUNK],
                     preferred_element_type=F32)
        f = (gate * jax.nn.sigmoid(gate) * up).astype(BF16)
        part = jnp.dot(f, w2_ref[cols, :], preferred_element_type=F32)
        if c == 0:
            acc_ref[...] = part
        else:
            acc_ref[...] += part
    x2 = x1_ref[...] + _rms(acc_ref[...]) * gpo_ref[...]
    xo_ref[...] = x2
    if fuse_next:
        _in_proj_body(x2, *next_in, *next_out)


def _out_ffn(yf, ya, ym, x, w_out_l, g_post_mix, g_pre_ffn, w1_l, w2_l, g_post_ffn, next_args, casts, *, l, tm):
    b, s, _ = x.shape
    nt = s // tm
    steps = b * nt
    assert tm == DFT_TILE and nt % 2 == 0
    tok = lambda w: pl.BlockSpec((None, tm, w), lambda i, j: (i, j, 0))
    vec = pl.BlockSpec((None, 1, D_MODEL), lambda i, j: (l, 0, 0))
    one = pl.Buffered(1)
    flat = lambda r, c: pl.BlockSpec((r, c), lambda i, j: (0, 0), pipeline_mode=one)
    in_specs = [
        pl.BlockSpec((None, tm, FOURIER_WIDTH),
                     lambda i, j: (j // (nt // 2), jnp.where(j < nt // 2, j, nt - 1 - j), i)),
        tok(ATTN_WIDTH), tok(MEM_WIDTH), tok(D_MODEL),
        flat(D_MODEL, D_MODEL), vec, vec, flat(D_MODEL, 2 * D_FF), flat(D_FF, D_MODEL), vec,
    ]
    out_specs = [tok(D_MODEL)]
    out_shape = [jax.ShapeDtypeStruct((b, s, D_MODEL), F32)]
    fuse_next = next_args is not None
    if fuse_next:
        nin, nout, nshape = _in_proj_specs(l + 1, b, s, tm)
        in_specs += nin
        out_specs += nout
        out_shape += nshape
    for w, layer, rows in casts:
        n_rows, n_cols = w.shape[1:]
        last = n_rows // rows - 1
        assert n_rows % rows == 0 and rows % 16 == 0 and last < steps
        in_specs.append(pl.BlockSpec(
            (None, rows, n_cols), lambda i, j, layer=layer, last=last: (layer, jnp.minimum(i * nt + j, last), 0)))
        out_specs.append(pl.BlockSpec(
            (rows, n_cols), lambda i, j, last=last: (jnp.minimum(i * nt + j, last), 0)))
        out_shape.append(jax.ShapeDtypeStruct((n_rows, n_cols), BF16))
    return pl.pallas_call(
        functools.partial(_out_ffn_kernel, fuse_next=fuse_next, n_cast=len(casts)),
        grid=(b, nt),
        in_specs=in_specs,
        out_specs=out_specs,
        out_shape=out_shape,
        scratch_shapes=[pltpu.VMEM((tm, D_MODEL), F32), pltpu.VMEM((tm, D_MODEL), BF16),
                        pltpu.VMEM((tm, D_MODEL), F32)],
        compiler_params=pltpu.CompilerParams(
            dimension_semantics=("arbitrary", "arbitrary"), vmem_limit_bytes=VMEM_LIMIT),
        name="out_ffn",
    )(yf, ya, ym, x, w_out_l, g_post_mix, g_pre_ffn, w1_l, w2_l, g_post_ffn,
      *(next_args or ()), *(w for w, _, _ in casts))


def kernel(x, mem, positions, g_pre_mix, w_in, w_fourier, sink, g_mem, w_mem_kv, g_grp,
           w_out, g_post_mix, g_pre_ffn, w_ffn_in, w_ffn_out, g_post_ffn):
    b, s, _ = x.shape
    assert s % 512 == 0 and s >= 3 * BLOCK and s == 64 * 64

    w_in_b = {0: w_in[0].astype(BF16), 1: w_in[1].astype(BF16)}
    w_out_b = {0: w_out[0].astype(BF16)}
    w1 = {0: w_ffn_in[0].astype(BF16)}
    w2 = {0: w_ffn_out[0].astype(BF16)}

    cos_t, sin_t = _rope_tables(positions)
    cmat, smat = _dft_matrices(s)
    ab = _fourier_fold(w_fourier)
    kmc, vmc = _mem_kv(mem, g_mem, w_mem_kv)

    vec3 = lambda g: g.reshape(DEPTH, 1, g.shape[-1])
    g_pre_mix, g_grp, g_post_mix, g_pre_ffn, g_post_ffn = map(
        vec3, (g_pre_mix, g_grp, g_post_mix, g_pre_ffn, g_post_ffn))

    in_args = lambda l: (g_pre_mix, w_in_b[l], ab, cos_t, sin_t)
    mix = _in_proj(x, *in_args(0), l=0, tm=1024)
    for l in range(DEPTH):
        q, k4, v4, zm, p2, q2 = mix
        ya, ym = _attention(sink, q, k4, v4, zm, kmc, vmc, g_grp, l=l, tq=2048)
        yf = _seq_dft(cmat, smat, p2, q2, g_grp, l=l)
        casts = []
        if l + 1 < DEPTH:
            casts += [(w_out, l + 1, 32), (w_ffn_in, l + 1, 32), (w_ffn_out, l + 1, 176)]
        if l + 2 < DEPTH:
            casts += [(w_in, l + 2, 32)]
        res = _out_ffn(yf, ya, ym, x, w_out_b[l], g_post_mix, g_pre_ffn, w1[l], w2[l], g_post_ffn,
                       in_args(l + 1) if l + 1 < DEPTH else None, casts, l=l, tm=512)
        x = res[0]
        if l + 1 < DEPTH:
            mix = res[1:7]
            w_out_b[l + 1], w1[l + 1], w2[l + 1] = res[7:10]
        if l + 2 < DEPTH:
            w_in_b[l + 2] = res[10]
    return x
```

```python
import functools
import math

import jax
import jax.numpy as jnp
from jax import lax
from jax.experimental import pallas as pl
from jax.experimental.pallas import tpu as pltpu

D_MODEL = 1024
DEPTH = 4
HEAD_DIM = 64
FOURIER_WIDTH = 256
FOURIER_CH = 64
N_FOURIER_GROUPS = 4
ATTN_WIDTH = 512
N_Q_HEADS = 8
N_KV_HEADS = 2
KV_WIDTH = 128
MEM_WIDTH = 256
N_MEM_HEADS = 4
IN_WIDTH = 1280
WINDOW = 128
BLOCK = 128
ROPE_THETA = 10000.0
D_FF = 2816
EPS = 1e-6
NEG_INF = -1e30

LANES = 128
BF16_SUBLANES = 16
TOKEN_TILE = 512
ATTN_TILE = 2048
VMEM_LIMIT = 56 * 1024 * 1024
FF_CHUNK = 256
N_FF_CHUNKS = D_FF // FF_CHUNK
LOG2E = math.log2(math.e)
Q_SCALE = HEAD_DIM ** -0.5 * LOG2E
DFT_TILE = TOKEN_TILE
DFT_ROWS = DFT_TILE + BF16_SUBLANES
PAIR_TILE = 256

F32 = jnp.float32
BF16 = jnp.bfloat16


def _rms(y):
    return y * lax.rsqrt(jnp.mean(y * y, axis=-1, keepdims=True) + EPS)


def _lane_low_half(shape):
    lane = lax.broadcasted_iota(jnp.int32, shape, len(shape) - 1)
    return (lane % LANES) < HEAD_DIM


def _rope_table_kernel(pos_ref, cos_ref, sin_ref):
    lane = lax.broadcasted_iota(jnp.int32, pos_ref.shape, 1)
    f = (lane % (HEAD_DIM // 2)).astype(F32)
    inv_freq = jnp.exp(-math.log(ROPE_THETA) * f * (2.0 / HEAD_DIM))
    ang = pos_ref[...] * inv_freq
    cos_ref[...] = jnp.cos(ang)
    sin_ref[...] = jnp.sin(ang)


def _rope_tables(positions):
    t = positions.size
    half = HEAD_DIM // 2
    rows = t * half // LANES
    pos = jnp.repeat(positions.reshape(t).astype(F32), half).reshape(rows, LANES)
    tr = TOKEN_TILE
    cos_c, sin_c = pl.pallas_call(
        _rope_table_kernel,
        grid=(rows // tr,),
        in_specs=[pl.BlockSpec((tr, LANES), lambda i: (i, 0))],
        out_specs=[pl.BlockSpec((tr, LANES), lambda i: (i, 0))] * 2,
        out_shape=[jax.ShapeDtypeStruct((rows, LANES), F32)] * 2,
        name="rope_tables",
    )(pos)
    return cos_c.reshape(t, half), sin_c.reshape(t, half)


def _dft_matrix_kernel(c_ref, s_ref, ac_ref, as_ref, bc_ref, bs_ref, *, n):
    t = pl.program_id(0)
    grp = ac_ref.shape[0]

    @pl.when(t == 0)
    def _():
        r = lax.broadcasted_iota(jnp.int32, ac_ref.shape, 0)
        k = lax.broadcasted_iota(jnp.int32, ac_ref.shape, 1)
        a = (((r * k) & (grp - 1)).astype(F32)) * (2.0 * math.pi / grp)
        b = (((r * k) & (n - 1)).astype(F32)) * (2.0 * math.pi / n)
        ac_ref[...] = jnp.cos(a)
        as_ref[...] = jnp.sin(a)
        bc_ref[...] = jnp.cos(b)
        bs_ref[...] = jnp.sin(b)

    for s in range(pl.cdiv(DFT_ROWS, grp)):
        nrow = min(grp, DFT_ROWS - s * grp)
        j1 = t * (DFT_TILE // grp) + s
        ca = ac_ref[pl.ds(j1, 1), :]
        sa = as_ref[pl.ds(j1, 1), :]
        cb = bc_ref[:nrow, :]
        sb = bs_ref[:nrow, :]
        c_ref[s * grp:s * grp + nrow, :] = (ca * cb - sa * sb).astype(BF16)
        s_ref[s * grp:s * grp + nrow, :] = (-(sa * cb + ca * sb)).astype(BF16)


def _dft_matrices(n):
    grp = 64
    nt = n // (2 * DFT_TILE)
    assert n == grp * grp and DFT_TILE % grp == 0 and DFT_TILE < DFT_ROWS <= DFT_TILE + grp
    return pl.pallas_call(
        functools.partial(_dft_matrix_kernel, n=n),
        grid=(nt,),
        out_specs=[pl.BlockSpec((None, DFT_ROWS, n // 2), lambda i: (i, 0, 0))] * 2,
        out_shape=[jax.ShapeDtypeStruct((nt, DFT_ROWS, n // 2), BF16)] * 2,
        scratch_shapes=[pltpu.VMEM((grp, n // 2), F32)] * 4,
        compiler_params=pltpu.CompilerParams(dimension_semantics=("arbitrary",), vmem_limit_bytes=VMEM_LIMIT),
        name="dft_matrices",
    )()


def _fourier_fold_kernel(w_ref, ab_ref):
    shp = (FOURIER_WIDTH, FOURIER_WIDTH)
    r = lax.broadcasted_iota(jnp.int32, shp, 0)
    c = lax.broadcasted_iota(jnp.int32, shp, 1)
    same = (r // FOURIER_CH) == (c // FOURIER_CH)
    ang = ((((r % FOURIER_CH) * (c % FOURIER_CH)) % FOURIER_CH).astype(F32)) * (2.0 * math.pi / FOURIER_CH)
    cm = jnp.where(same, jnp.cos(ang), 0.0)
    sm = jnp.where(same, jnp.sin(ang), 0.0)
    w = w_ref[...]
    a = jnp.dot(cm, w, preferred_element_type=F32, precision=lax.Precision.HIGHEST)
    b = jnp.dot(sm, w, preferred_element_type=F32, precision=lax.Precision.HIGHEST)
    ab_ref[:, :FOURIER_WIDTH] = a.astype(BF16)
    ab_ref[:, FOURIER_WIDTH:] = b.astype(BF16)


def _fourier_fold(w_fourier):
    eye = jnp.eye(N_FOURIER_GROUPS, dtype=F32)
    w_bd = jnp.einsum("lgce,gh->lgche", w_fourier, eye).reshape(DEPTH, FOURIER_WIDTH, FOURIER_WIDTH)
    return pl.pallas_call(
        _fourier_fold_kernel,
        grid=(DEPTH,),
        in_specs=[pl.BlockSpec((None, FOURIER_WIDTH, FOURIER_WIDTH), lambda l: (l, 0, 0))],
        out_specs=pl.BlockSpec((None, FOURIER_WIDTH, 2 * FOURIER_WIDTH), lambda l: (l, 0, 0)),
        out_shape=jax.ShapeDtypeStruct((DEPTH, FOURIER_WIDTH, 2 * FOURIER_WIDTH), BF16),
        name="fourier_fold",
    )(w_bd)


def _mem_kv_kernel(mem_ref, g_ref, w_ref, kc_ref, vc_ref):
    nb, _, two_m, _ = kc_ref.shape
    m = two_m // 2
    h = (_rms(mem_ref[...]) * g_ref[...]).astype(BF16)
    mkv = jnp.dot(h, w_ref[...].astype(BF16), preferred_element_type=F32)
    low = _lane_low_half((m, LANES))
    for i in range(nb):
        rows = slice(i * m, (i + 1) * m)
        for a in range(N_MEM_HEADS // 2):
            kp = mkv[rows, a * LANES:(a + 1) * LANES]
            vp = mkv[rows, MEM_WIDTH + a * LANES:MEM_WIDTH + (a + 1) * LANES]
            kc_ref[i, a, :m, :] = jnp.where(low, kp, 0.0).astype(BF16)
            kc_ref[i, a, m:, :] = jnp.where(low, 0.0, kp).astype(BF16)
            vc_ref[i, a, :m, :] = jnp.where(low, vp, 0.0).astype(BF16)
            vc_ref[i, a, m:, :] = jnp.where(low, 0.0, vp).astype(BF16)


def _mem_kv(mem, g_mem, w_mem_kv):
    b, m, _ = mem.shape
    npair = N_MEM_HEADS // 2
    out = jax.ShapeDtypeStruct((DEPTH, b, npair, 2 * m, LANES), BF16)
    spec = pl.BlockSpec((None, b, npair, 2 * m, LANES), lambda l: (l, 0, 0, 0, 0))
    return pl.pallas_call(
        _mem_kv_kernel,
        grid=(DEPTH,),
        in_specs=[
            pl.BlockSpec((b * m, D_MODEL), lambda l: (0, 0)),
            pl.BlockSpec((None, 1, D_MODEL), lambda l: (l, 0, 0)),
            pl.BlockSpec((None, D_MODEL, 2 * MEM_WIDTH), lambda l: (l, 0, 0)),
        ],
        out_specs=[spec, spec],
        out_shape=[out, out],
        name="mem_kv",
    )(mem.reshape(b * m, D_MODEL), g_mem.reshape(DEPTH, 1, D_MODEL), w_mem_kv)


def _in_proj_body(xv, g_ref, w_ref, ab_ref, cos_ref, sin_ref, q_ref, k_ref, v_ref, zm_ref, p_ref, qq_ref):
    h = (_rms(xv) * g_ref[...]).astype(BF16)
    z = jnp.dot(h, w_ref[...], preferred_element_type=F32)
    tm = z.shape[0]

    zf = z[:, :FOURIER_WIDTH].astype(BF16)
    pq = jnp.dot(zf, ab_ref[...], preferred_element_type=F32)
    p_ref[...] = pq[:, :FOURIER_WIDTH].astype(BF16)
    qq_ref[...] = pq[:, FOURIER_WIDTH:].astype(BF16)

    c32 = cos_ref[...]
    s32 = sin_ref[...]
    cos = jnp.concatenate([c32, c32, c32, c32], axis=1)
    sin = jnp.concatenate([-s32, s32, -s32, s32], axis=1)
    first = (lax.broadcasted_iota(jnp.int32, (tm, LANES), 1) % HEAD_DIM) < (HEAD_DIM // 2)

    def rope(t, c, s):
        rot = jnp.where(first, pltpu.roll(t, LANES - HEAD_DIM // 2, 1), pltpu.roll(t, HEAD_DIM // 2, 1))
        return t * c + rot * s

    q0 = FOURIER_WIDTH
    cq = cos * Q_SCALE
    sq = sin * Q_SCALE
    for c in range(ATTN_WIDTH // LANES):
        t = z[:, q0 + c * LANES:q0 + (c + 1) * LANES]
        q_ref[:, c * LANES:(c + 1) * LANES] = rope(t, cq, sq).astype(BF16)

    k0 = q0 + ATTN_WIDTH
    low = _lane_low_half((tm, LANES))
    kk = rope(z[:, k0:k0 + KV_WIDTH], cos, sin)
    vv = z[:, k0 + KV_WIDTH:k0 + 2 * KV_WIDTH]
    for src, dst in ((kk, k_ref), (vv, v_ref)):
        sw = pltpu.roll(src, HEAD_DIM, 1)
        dst[:, 0 * LANES:1 * LANES] = jnp.where(low, src, 0.0).astype(BF16)
        dst[:, 1 * LANES:2 * LANES] = jnp.where(low, 0.0, sw).astype(BF16)
        dst[:, 2 * LANES:3 * LANES] = jnp.where(low, sw, 0.0).astype(BF16)
        dst[:, 3 * LANES:4 * LANES] = jnp.where(low, 0.0, src).astype(BF16)

    m0 = k0 + 2 * KV_WIDTH
    zm_ref[...] = (z[:, m0:m0 + MEM_WIDTH] * Q_SCALE).astype(BF16)


def _in_proj_kernel(x_ref, *refs):
    _in_proj_body(x_ref[...], *refs)


def _in_proj_specs(l, b, s, tm):
    nt = s // tm
    tok = lambda w: pl.BlockSpec((None, tm, w), lambda i, j: (i, j, 0))
    seq_major = pl.BlockSpec((tm, FOURIER_WIDTH), lambda i, j: (j, i))
    rope_spec = pl.BlockSpec((tm, HEAD_DIM // 2), lambda i, j: (i * nt + j, 0))
    in_specs = [
        pl.BlockSpec((None, 1, D_MODEL), lambda i, j: (l, 0, 0)),
        pl.BlockSpec((D_MODEL, IN_WIDTH), lambda i, j: (0, 0), pipeline_mode=pl.Buffered(1)),
        pl.BlockSpec((None, FOURIER_WIDTH, 2 * FOURIER_WIDTH), lambda i, j: (l, 0, 0),
                     pipeline_mode=pl.Buffered(1)),
        rope_spec, rope_spec,
    ]
    out_specs = [tok(ATTN_WIDTH), tok(4 * LANES), tok(4 * LANES), tok(MEM_WIDTH), seq_major, seq_major]
    out_shape = [
        jax.ShapeDtypeStruct((b, s, ATTN_WIDTH), BF16),
        jax.ShapeDtypeStruct((b, s, 4 * LANES), BF16),
        jax.ShapeDtypeStruct((b, s, 4 * LANES), BF16),
        jax.ShapeDtypeStruct((b, s, MEM_WIDTH), BF16),
        jax.ShapeDtypeStruct((s, b * FOURIER_WIDTH), BF16),
        jax.ShapeDtypeStruct((s, b * FOURIER_WIDTH), BF16),
    ]
    return in_specs, out_specs, out_shape


def _in_proj(x, g_pre_mix, w_in_b, ab, cos_t, sin_t, *, l, tm):
    b, s, _ = x.shape
    in_specs, out_specs, out_shape = _in_proj_specs(l, b, s, tm)
    return pl.pallas_call(
        _in_proj_kernel,
        grid=(b, s // tm),
        in_specs=[pl.BlockSpec((None, tm, D_MODEL), lambda i, j: (i, j, 0))] + in_specs,
        out_specs=out_specs,
        out_shape=out_shape,
        compiler_params=pltpu.CompilerParams(
            dimension_semantics=("arbitrary", "arbitrary"), vmem_limit_bytes=VMEM_LIMIT),
        name="in_proj",
    )(x, g_pre_mix, w_in_b, ab, cos_t, sin_t)


def _softmax_parts(sc, sink):
    m = jnp.max(sc, axis=-1, keepdims=True)
    if sink is None:
        return jnp.exp2(sc - m).astype(BF16), None
    m = jnp.maximum(m, sink)
    return jnp.exp2(sc - m).astype(BF16), jnp.exp2(sink - m)


def _head_indicator(rows):
    low = _lane_low_half((rows, LANES))
    top = jnp.where(low, 1.0, 0.0).astype(BF16)
    bot = jnp.where(low, 0.0, 1.0).astype(BF16)
    return jnp.concatenate([top, bot], axis=0)


def _attn_kernel(sink_ref, q_ref, k_ref, v_ref, zm_ref, kmc_ref, vmc_ref, gg_ref,
                 ya_ref, ym_ref, *, seq, layer):
    ga = gg_ref[:, FOURIER_WIDTH:FOURIER_WIDTH + ATTN_WIDTH]
    gm = gg_ref[:, FOURIER_WIDTH + ATTN_WIDTH:]
    tq = q_ref.shape[0]
    nblk = tq // BLOCK
    band = 3 * BLOCK
    t = pl.program_id(1)
    rel = (lax.broadcasted_iota(jnp.int32, (BLOCK, band), 0)
           - lax.broadcasted_iota(jnp.int32, (BLOCK, band), 1))
    low = _lane_low_half((BLOCK, LANES))
    pairs_per_kv = N_Q_HEADS // N_KV_HEADS // 2
    ind_band = _head_indicator(band)

    for jj in range(nblk):
        blk = t * nblk + jj
        s0 = pl.multiple_of(jnp.clip((blk - 1) * BLOCK, 0, seq - band), BLOCK)
        d = blk * BLOCK - s0
        valid = jnp.abs(rel + d) <= WINDOW
        rows = pl.ds(jj * BLOCK, BLOCK)
        outs = []
        for g in range(N_KV_HEADS):
            kcat = jnp.concatenate(
                [k_ref[pl.ds(s0, band), (2 * g) * LANES:(2 * g + 1) * LANES],
                 k_ref[pl.ds(s0, band), (2 * g + 1) * LANES:(2 * g + 2) * LANES]], axis=0)
            vcat = jnp.concatenate(
                [v_ref[pl.ds(s0, band), (2 * g) * LANES:(2 * g + 1) * LANES],
                 v_ref[pl.ds(s0, band), (2 * g + 1) * LANES:(2 * g + 2) * LANES]], axis=0)
            qg = jnp.concatenate(
                [q_ref[rows, (pairs_per_kv * g + p) * LANES:(pairs_per_kv * g + p + 1) * LANES]
                 for p in range(pairs_per_kv)], axis=0)
            sc = lax.dot_general(qg, kcat, (((1,), (1,)), ((), ())), preferred_element_type=F32)
            probs, sink_terms = [], []
            for p in range(pairs_per_kv):
                row_p, row_s = [], []
                for e in range(2):
                    head = (N_Q_HEADS // N_KV_HEADS) * g + 2 * p + e
                    sb = sc[p * BLOCK:(p + 1) * BLOCK, e * band:(e + 1) * band]
                    sb = jnp.where(valid, sb, NEG_INF)
                    pe, st = _softmax_parts(sb, sink_ref[layer, head] * LOG2E)
                    row_p.append(pe)
                    row_s.append(st)
                probs.append(jnp.concatenate(row_p, axis=1))
                sink_terms.append(row_s)
            pm = jnp.concatenate(probs, axis=0)
            o = jnp.dot(pm, jnp.concatenate([vcat, ind_band], axis=1), preferred_element_type=F32)
            for p in range(pairs_per_kv):
                op = o[p * BLOCK:(p + 1) * BLOCK, :]
                den = op[:, LANES:] + jnp.where(low, sink_terms[p][0], sink_terms[p][1])
                outs.append(op[:, :LANES] / den)
        y = jnp.concatenate(outs, axis=1)
        ya_ref[rows, :] = (_rms(y) * ga).astype(BF16)

    nmem = kmc_ref.shape[1] // 2
    ind_mem = _head_indicator(nmem)
    outs = []
    for a in range(N_MEM_HEADS // 2):
        qa = zm_ref[:, a * LANES:(a + 1) * LANES]
        sc = lax.dot_general(qa, kmc_ref[a], (((1,), (1,)), ((), ())), preferred_element_type=F32)
        p0, _ = _softmax_parts(sc[:, :nmem], None)
        p1, _ = _softmax_parts(sc[:, nmem:], None)
        o = jnp.dot(jnp.concatenate([p0, p1], axis=1), jnp.concatenate([vmc_ref[a], ind_mem], axis=1),
                    preferred_element_type=F32)
        outs.append(o[:, :LANES] / o[:, LANES:])
    ym = jnp.concatenate(outs, axis=1)
    ym_ref[...] = (_rms(ym) * gm).astype(BF16)


def _attention(sink, q, k4, v4, zm, kmc, vmc, g_grp, *, l, tq):
    b, s, _ = q.shape
    nt = s // tq
    npair, m2, _ = kmc.shape[2:]
    tok = lambda w: pl.BlockSpec((None, tq, w), lambda i, j: (i, j, 0))
    per_batch = lambda w: pl.BlockSpec((None, s, w), lambda i, j: (i, 0, 0))
    memspec = pl.BlockSpec((None, None, npair, m2, LANES), lambda i, j: (l, i, 0, 0, 0))
    return pl.pallas_call(
        functools.partial(_attn_kernel, seq=s, layer=l),
        grid=(b, nt),
        in_specs=[
            pl.BlockSpec(memory_space=pltpu.SMEM),
            tok(ATTN_WIDTH), per_batch(4 * LANES), per_batch(4 * LANES), tok(MEM_WIDTH),
            memspec, memspec,
            pl.BlockSpec((None, 1, D_MODEL), lambda i, j: (l, 0, 0)),
        ],
        out_specs=[tok(ATTN_WIDTH), tok(MEM_WIDTH)],
        out_shape=[jax.ShapeDtypeStruct((b, s, ATTN_WIDTH), BF16),
                   jax.ShapeDtypeStruct((b, s, MEM_WIDTH), BF16)],
        compiler_params=pltpu.CompilerParams(
            dimension_semantics=("arbitrary", "arbitrary"), vmem_limit_bytes=VMEM_LIMIT),
        name="attention",
    )(sink, q, k4, v4, zm, kmc, vmc, g_grp)


def _exchange_matrix(size):
    ra = lax.broadcasted_iota(jnp.int32, (size, size), 0)
    rr = lax.broadcasted_iota(jnp.int32, (size, size), 1)
    return jnp.where(rr == size - ra, 1.0, 0.0).astype(BF16)


def _seq_dft_kernel(c_ref, s_ref, p_ref, q_ref, g_ref, y_ref, pp_ref, qm_ref, *, scale):
    n = p_ref.shape[0]
    half = n // 2

    @pl.when(pl.program_id(0) == 0)
    def _():
        flip = _exchange_matrix(PAIR_TILE)
        first = lax.broadcasted_iota(jnp.int32, (PAIR_TILE, p_ref.shape[1]), 0) == 0
        for a in range(half // PAIR_TILE):
            lo = a * PAIR_TILE
            src = n - lo - PAIR_TILE
            for x_ref, dst, sgn in ((p_ref, pp_ref, 1.0), (q_ref, qm_ref, -1.0)):
                rev = jnp.dot(flip, x_ref[src:src + PAIR_TILE, :], preferred_element_type=F32)
                if a > 0:
                    edge = x_ref[src + PAIR_TILE:src + PAIR_TILE + BF16_SUBLANES, :][0:1].astype(F32)
                    rev = jnp.where(first, edge, rev)
                dst[lo:lo + PAIR_TILE, :] = (x_ref[lo:lo + PAIR_TILE, :].astype(F32) + sgn * rev).astype(BF16)

    def group_norm(y):
        nb = y.shape[1] // FOURIER_WIDTH
        return jnp.concatenate(
            [(_rms(y[:, b * FOURIER_WIDTH:(b + 1) * FOURIER_WIDTH]) * g_ref[:, :FOURIER_WIDTH]).astype(BF16)
             for b in range(nb)], axis=1)

    mid = p_ref[half:half + BF16_SUBLANES, :][0:1].astype(F32)
    odd = (lax.broadcasted_iota(jnp.int32, (DFT_ROWS, 1), 0) & 1) == 1
    a1 = jnp.dot(c_ref[...], pp_ref[...], preferred_element_type=F32)
    a1 = (a1 + jnp.where(odd, -mid, mid)) * scale
    a2 = jnp.dot(s_ref[...], qm_ref[...], preferred_element_type=F32) * scale
    y_ref[0] = group_norm((a1 + a2)[:DFT_TILE])
    hi = group_norm(a1 - a2)
    mirrored = jnp.dot(_exchange_matrix(DFT_TILE), hi[:DFT_TILE], preferred_element_type=F32).astype(BF16)
    first = lax.broadcasted_iota(jnp.int32, mirrored.shape, 0) == 0
    y_ref[1] = jnp.where(first, hi[DFT_TILE:DFT_TILE + 1], mirrored)


def _seq_dft(cmat, smat, p2, q2, g_grp, *, l):
    s, w = p2.shape
    nt = cmat.shape[0]
    scale = 1.0 / math.sqrt(s * FOURIER_CH)
    resident = pl.BlockSpec((s, w), lambda i: (0, 0), pipeline_mode=pl.Buffered(1))
    half = pl.BlockSpec((None, DFT_ROWS, s // 2), lambda i: (i, 0, 0))
    return pl.pallas_call(
        functools.partial(_seq_dft_kernel, scale=scale),
        grid=(nt,),
        in_specs=[half, half, resident, resident, pl.BlockSpec((None, 1, D_MODEL), lambda i: (l, 0, 0))],
        out_specs=pl.BlockSpec((2, DFT_TILE, w), lambda i: (0, i, 0)),
        out_shape=jax.ShapeDtypeStruct((2, s // 2, w), BF16),
        scratch_shapes=[pltpu.VMEM((s // 2, w), BF16), pltpu.VMEM((s // 2, w), BF16)],
        compiler_params=pltpu.CompilerParams(
            dimension_semantics=("arbitrary",), vmem_limit_bytes=VMEM_LIMIT),
        name="seq_dft",
    )(cmat, smat, p2, q2, g_grp)


def _out_ffn_kernel(yf_ref, ya_ref, ym_ref, x_ref, wo_ref, gpm_ref, gpf_ref, w1_ref, w2_ref, gpo_ref,
                    *rest, fuse_next, n_cast):
    n_next_in = 5 if fuse_next else 0
    n_next_out = 6 if fuse_next else 0
    next_in = rest[:n_next_in]
    cast_src = rest[n_next_in:n_next_in + n_cast]
    outs = rest[n_next_in + n_cast:-3]
    xo_ref, next_out, cast_dst = outs[0], outs[1:1 + n_next_out], outs[1 + n_next_out:]
    x1_ref, h_ref, acc_ref = rest[-3:]
    for src, dst in zip(cast_src, cast_dst):
        dst[...] = src[...].astype(BF16)
    ycat = jnp.concatenate([yf_ref[...], ya_ref[...], ym_ref[...]], axis=1)
    y = jnp.dot(ycat, wo_ref[...], preferred_element_type=F32)
    x1 = x_ref[...] + _rms(y) * gpm_ref[...]
    x1_ref[...] = x1
    h_ref[...] = (_rms(x1) * gpf_ref[...]).astype(BF16)
    for c in range(N_FF_CHUNKS):
        cols = slice(c * FF_CHUNK, (c + 1) * FF_CHUNK)
        gate = jnp.dot(h_ref[...], w1_ref[:, cols], preferred_element_type=F32)
        up = jnp.dot(h_ref[...], w1_ref[:, D_FF + c * FF_CHUNK:D_FF + (c + 1) * FF_CHUNK],
                     preferred_element_type=F32)
        f = (gate * jax.nn.sigmoid(gate) * up).astype(BF16)
        part = jnp.dot(f, w2_ref[cols, :], preferred_element_type=F32)
        if c == 0:
            acc_ref[...] = part
        else:
            acc_ref[...] += part
    x2 = x1_ref[...] + _rms(acc_ref[...]) * gpo_ref[...]
    xo_ref[...] = x2
    if fuse_next:
        _in_proj_body(x2, *next_in, *next_out)


def _cast_rows(n_rows, steps):
    rows = BF16_SUBLANES * pl.cdiv(n_rows, BF16_SUBLANES * steps)
    while n_rows % rows:
        rows += BF16_SUBLANES
    return rows


def _out_ffn(yf, ya, ym, x, w_out_l, g_post_mix, g_pre_ffn, w1_l, w2_l, g_post_ffn, next_args, casts, *, l, tm):
    b, s, _ = x.shape
    nt = s // tm
    steps = b * nt
    assert tm == DFT_TILE and nt % 2 == 0
    tok = lambda w: pl.BlockSpec((None, tm, w), lambda i, j: (i, j, 0))
    vec = pl.BlockSpec((None, 1, D_MODEL), lambda i, j: (l, 0, 0))
    one = pl.Buffered(1)
    flat = lambda r, c: pl.BlockSpec((r, c), lambda i, j: (0, 0), pipeline_mode=one)
    in_specs = [
        pl.BlockSpec((None, tm, FOURIER_WIDTH),
                     lambda i, j: (j // (nt // 2), jnp.where(j < nt // 2, j, nt - 1 - j), i)),
        tok(ATTN_WIDTH), tok(MEM_WIDTH), tok(D_MODEL),
        flat(D_MODEL, D_MODEL), vec, vec, flat(D_MODEL, 2 * D_FF), flat(D_FF, D_MODEL), vec,
    ]
    out_specs = [tok(D_MODEL)]
    out_shape = [jax.ShapeDtypeStruct((b, s, D_MODEL), F32)]
    fuse_next = next_args is not None
    if fuse_next:
        nin, nout, nshape = _in_proj_specs(l + 1, b, s, tm)
        in_specs += nin
        out_specs += nout
        out_shape += nshape
    for w, layer in casts:
        n_rows, n_cols = w.shape[1:]
        rows = _cast_rows(n_rows, steps)
        last = n_rows // rows - 1
        in_specs.append(pl.BlockSpec(
            (None, rows, n_cols), lambda i, j, layer=layer, last=last: (layer, jnp.minimum(i * nt + j, last), 0)))
        out_specs.append(pl.BlockSpec(
            (rows, n_cols), lambda i, j, last=last: (jnp.minimum(i * nt + j, last), 0)))
        out_shape.append(jax.ShapeDtypeStruct((n_rows, n_cols), BF16))
    return pl.pallas_call(
        functools.partial(_out_ffn_kernel, fuse_next=fuse_next, n_cast=len(casts)),
        grid=(b, nt),
        in_specs=in_specs,
        out_specs=out_specs,
        out_shape=out_shape,
        scratch_shapes=[pltpu.VMEM((tm, D_MODEL), F32), pltpu.VMEM((tm, D_MODEL), BF16),
                        pltpu.VMEM((tm, D_MODEL), F32)],
        compiler_params=pltpu.CompilerParams(
            dimension_semantics=("arbitrary", "arbitrary"), vmem_limit_bytes=VMEM_LIMIT),
        name="out_ffn",
    )(yf, ya, ym, x, w_out_l, g_post_mix, g_pre_ffn, w1_l, w2_l, g_post_ffn,
      *(next_args or ()), *(w for w, _ in casts))


def kernel(x, mem, positions, g_pre_mix, w_in, w_fourier, sink, g_mem, w_mem_kv, g_grp,
           w_out, g_post_mix, g_pre_ffn, w_ffn_in, w_ffn_out, g_post_ffn):
    b, s, _ = x.shape
    assert s % ATTN_TILE == 0 and s >= 3 * BLOCK

    w_in_b = {0: w_in[0].astype(BF16), 1: w_in[1].astype(BF16)}
    w_out_b = {0: w_out[0].astype(BF16)}
    w1 = {0: w_ffn_in[0].astype(BF16)}
    w2 = {0: w_ffn_out[0].astype(BF16)}

    cos_t, sin_t = _rope_tables(positions)
    cmat, smat = _dft_matrices(s)
    ab = _fourier_fold(w_fourier)
    kmc, vmc = _mem_kv(mem, g_mem, w_mem_kv)

    vec3 = lambda g: g.reshape(DEPTH, 1, g.shape[-1])
    g_pre_mix, g_grp, g_post_mix, g_pre_ffn, g_post_ffn = map(
        vec3, (g_pre_mix, g_grp, g_post_mix, g_pre_ffn, g_post_ffn))

    in_args = lambda l: (g_pre_mix, w_in_b[l], ab, cos_t, sin_t)
    mix = _in_proj(x, *in_args(0), l=0, tm=2 * TOKEN_TILE)
    for l in range(DEPTH):
        q, k4, v4, zm, p2, q2 = mix
        ya, ym = _attention(sink, q, k4, v4, zm, kmc, vmc, g_grp, l=l, tq=ATTN_TILE)
        yf = _seq_dft(cmat, smat, p2, q2, g_grp, l=l)
        casts = []
        if l + 1 < DEPTH:
            casts += [(w_out, l + 1), (w_ffn_in, l + 1), (w_ffn_out, l + 1)]
        if l + 2 < DEPTH:
            casts += [(w_in, l + 2)]
        res = _out_ffn(yf, ya, ym, x, w_out_b[l], g_post_mix, g_pre_ffn, w1[l], w2[l], g_post_ffn,
                       in_args(l + 1) if l + 1 < DEPTH else None, casts, l=l, tm=TOKEN_TILE)
        x = res[0]
        if l + 1 < DEPTH:
            mix = res[1:7]
            w_out_b[l + 1], w1[l + 1], w2[l + 1] = res[7:10]
        if l + 2 < DEPTH:
            w_in_b[l + 2] = res[10]
    return x
```

```python
import functools
import math

import jax
import jax.numpy as jnp
from jax import lax
from jax.experimental import pallas as pl
from jax.experimental.pallas import tpu as pltpu

D_MODEL = 1024
DEPTH = 4
HEAD_DIM = 64
FOURIER_WIDTH = 256
FOURIER_CH = 64
N_FOURIER_GROUPS = 4
ATTN_WIDTH = 512
N_Q_HEADS = 8
N_KV_HEADS = 2
KV_WIDTH = 128
MEM_WIDTH = 256
N_MEM_HEADS = 4
IN_WIDTH = 1280
WINDOW = 128
BLOCK = 128
ROPE_THETA = 10000.0
D_FF = 2816
EPS = 1e-6
NEG_INF = -1e30

LANES = 128
BF16_SUBLANES = 16
TOKEN_TILE = 512
ATTN_TILE = 2048
VMEM_LIMIT = 56 * 1024 * 1024
FF_CHUNK = 256
N_FF_CHUNKS = D_FF // FF_CHUNK
LOG2E = math.log2(math.e)
Q_SCALE = HEAD_DIM ** -0.5 * LOG2E
DFT_TILE = TOKEN_TILE
DFT_ROWS = DFT_TILE + BF16_SUBLANES
PAIR_TILE = 256

F32 = jnp.float32
BF16 = jnp.bfloat16


def _rms(y):
    return y * lax.rsqrt(jnp.mean(y * y, axis=-1, keepdims=True) + EPS)


def _lane_low_half(shape):
    lane = lax.broadcasted_iota(jnp.int32, shape, len(shape) - 1)
    return (lane % LANES) < HEAD_DIM


def _rope_table_kernel(pos_ref, cos_ref, sin_ref):
    lane = lax.broadcasted_iota(jnp.int32, pos_ref.shape, 1)
    f = (lane % (HEAD_DIM // 2)).astype(F32)
    inv_freq = jnp.exp(-math.log(ROPE_THETA) * f * (2.0 / HEAD_DIM))
    ang = pos_ref[...] * inv_freq
    cos_ref[...] = jnp.cos(ang)
    sin_ref[...] = jnp.sin(ang)


def _rope_tables(positions):
    t = positions.size
    half = HEAD_DIM // 2
    rows = t * half // LANES
    pos = jnp.repeat(positions.reshape(t).astype(F32), half).reshape(rows, LANES)
    tr = TOKEN_TILE
    cos_c, sin_c = pl.pallas_call(
        _rope_table_kernel,
        grid=(rows // tr,),
        in_specs=[pl.BlockSpec((tr, LANES), lambda i: (i, 0))],
        out_specs=[pl.BlockSpec((tr, LANES), lambda i: (i, 0))] * 2,
        out_shape=[jax.ShapeDtypeStruct((rows, LANES), F32)] * 2,
        name="rope_tables",
    )(pos)
    return cos_c.reshape(t, half), sin_c.reshape(t, half)


def _dft_matrix_kernel(c_ref, s_ref, ac_ref, as_ref, bc_ref, bs_ref, *, n):
    t = pl.program_id(0)
    grp = ac_ref.shape[0]

    @pl.when(t == 0)
    def _():
        r = lax.broadcasted_iota(jnp.int32, ac_ref.shape, 0)
        k = lax.broadcasted_iota(jnp.int32, ac_ref.shape, 1)
        a = (((r * k) & (grp - 1)).astype(F32)) * (2.0 * math.pi / grp)
        b = (((r * k) & (n - 1)).astype(F32)) * (2.0 * math.pi / n)
        ac_ref[...] = jnp.cos(a)
        as_ref[...] = jnp.sin(a)
        bc_ref[...] = jnp.cos(b)
        bs_ref[...] = jnp.sin(b)

    for s in range(pl.cdiv(DFT_ROWS, grp)):
        nrow = min(grp, DFT_ROWS - s * grp)
        j1 = t * (DFT_TILE // grp) + s
        ca = ac_ref[pl.ds(j1, 1), :]
        sa = as_ref[pl.ds(j1, 1), :]
        cb = bc_ref[:nrow, :]
        sb = bs_ref[:nrow, :]
        c_ref[s * grp:s * grp + nrow, :] = (ca * cb - sa * sb).astype(BF16)
        s_ref[s * grp:s * grp + nrow, :] = (-(sa * cb + ca * sb)).astype(BF16)


def _dft_matrices(n):
    grp = 64
    nt = n // (2 * DFT_TILE)
    assert n == grp * grp and DFT_TILE % grp == 0 and DFT_TILE < DFT_ROWS <= DFT_TILE + grp
    return pl.pallas_call(
        functools.partial(_dft_matrix_kernel, n=n),
        grid=(nt,),
        out_specs=[pl.BlockSpec((None, DFT_ROWS, n // 2), lambda i: (i, 0, 0))] * 2,
        out_shape=[jax.ShapeDtypeStruct((nt, DFT_ROWS, n // 2), BF16)] * 2,
        scratch_shapes=[pltpu.VMEM((grp, n // 2), F32)] * 4,
        compiler_params=pltpu.CompilerParams(dimension_semantics=("arbitrary",), vmem_limit_bytes=VMEM_LIMIT),
        name="dft_matrices",
    )()


def _fourier_fold_kernel(w_ref, ab_ref):
    shp = (FOURIER_WIDTH, FOURIER_WIDTH)
    r = lax.broadcasted_iota(jnp.int32, shp, 0)
    c = lax.broadcasted_iota(jnp.int32, shp, 1)
    same = (r // FOURIER_CH) == (c // FOURIER_CH)
    ang = ((((r % FOURIER_CH) * (c % FOURIER_CH)) % FOURIER_CH).astype(F32)) * (2.0 * math.pi / FOURIER_CH)
    cm = jnp.where(same, jnp.cos(ang), 0.0)
    sm = jnp.where(same, jnp.sin(ang), 0.0)
    w = w_ref[...]
    a = jnp.dot(cm, w, preferred_element_type=F32, precision=lax.Precision.HIGHEST)
    b = jnp.dot(sm, w, preferred_element_type=F32, precision=lax.Precision.HIGHEST)
    ab_ref[:, :FOURIER_WIDTH] = a.astype(BF16)
    ab_ref[:, FOURIER_WIDTH:] = b.astype(BF16)


def _fourier_fold(w_fourier):
    eye = jnp.eye(N_FOURIER_GROUPS, dtype=F32)
    w_bd = jnp.einsum("lgce,gh->lgche", w_fourier, eye).reshape(DEPTH, FOURIER_WIDTH, FOURIER_WIDTH)
    return pl.pallas_call(
        _fourier_fold_kernel,
        grid=(DEPTH,),
        in_specs=[pl.BlockSpec((None, FOURIER_WIDTH, FOURIER_WIDTH), lambda l: (l, 0, 0))],
        out_specs=pl.BlockSpec((None, FOURIER_WIDTH, 2 * FOURIER_WIDTH), lambda l: (l, 0, 0)),
        out_shape=jax.ShapeDtypeStruct((DEPTH, FOURIER_WIDTH, 2 * FOURIER_WIDTH), BF16),
        name="fourier_fold",
    )(w_bd)


def _mem_kv_kernel(mem_ref, g_ref, w_ref, kc_ref, vc_ref):
    nb, _, two_m, _ = kc_ref.shape
    m = two_m // 2
    h = (_rms(mem_ref[...]) * g_ref[...]).astype(BF16)
    mkv = jnp.dot(h, w_ref[...].astype(BF16), preferred_element_type=F32)
    low = _lane_low_half((m, LANES))
    for i in range(nb):
        rows = slice(i * m, (i + 1) * m)
        for a in range(N_MEM_HEADS // 2):
            kp = mkv[rows, a * LANES:(a + 1) * LANES]
            vp = mkv[rows, MEM_WIDTH + a * LANES:MEM_WIDTH + (a + 1) * LANES]
            kc_ref[i, a, :m, :] = jnp.where(low, kp, 0.0).astype(BF16)
            kc_ref[i, a, m:, :] = jnp.where(low, 0.0, kp).astype(BF16)
            vc_ref[i, a, :m, :] = jnp.where(low, vp, 0.0).astype(BF16)
            vc_ref[i, a, m:, :] = jnp.where(low, 0.0, vp).astype(BF16)


def _mem_kv(mem, g_mem, w_mem_kv):
    b, m, _ = mem.shape
    npair = N_MEM_HEADS // 2
    out = jax.ShapeDtypeStruct((DEPTH, b, npair, 2 * m, LANES), BF16)
    spec = pl.BlockSpec((None, b, npair, 2 * m, LANES), lambda l: (l, 0, 0, 0, 0))
    return pl.pallas_call(
        _mem_kv_kernel,
        grid=(DEPTH,),
        in_specs=[
            pl.BlockSpec((b * m, D_MODEL), lambda l: (0, 0)),
            pl.BlockSpec((None, 1, D_MODEL), lambda l: (l, 0, 0)),
            pl.BlockSpec((None, D_MODEL, 2 * MEM_WIDTH), lambda l: (l, 0, 0)),
        ],
        out_specs=[spec, spec],
        out_shape=[out, out],
        name="mem_kv",
    )(mem.reshape(b * m, D_MODEL), g_mem.reshape(DEPTH, 1, D_MODEL), w_mem_kv)


def _in_proj_body(xv, g_ref, w_ref, ab_ref, cos_ref, sin_ref, q_ref, k_ref, v_ref, zm_ref, p_ref, qq_ref):
    h = (_rms(xv) * g_ref[...]).astype(BF16)
    z = jnp.dot(h, w_ref[...], preferred_element_type=F32)
    tm = z.shape[0]

    zf = z[:, :FOURIER_WIDTH].astype(BF16)
    pq = jnp.dot(zf, ab_ref[...], preferred_element_type=F32)
    p_ref[...] = pq[:, :FOURIER_WIDTH].astype(BF16)
    qq_ref[...] = pq[:, FOURIER_WIDTH:].astype(BF16)

    c32 = cos_ref[...]
    s32 = sin_ref[...]
    cos = jnp.concatenate([c32, c32, c32, c32], axis=1)
    sin = jnp.concatenate([-s32, s32, -s32, s32], axis=1)
    first = (lax.broadcasted_iota(jnp.int32, (tm, LANES), 1) % HEAD_DIM) < (HEAD_DIM // 2)

    def rope(t, c, s):
        rot = jnp.where(first, pltpu.roll(t, LANES - HEAD_DIM // 2, 1), pltpu.roll(t, HEAD_DIM // 2, 1))
        return t * c + rot * s

    q0 = FOURIER_WIDTH
    cq = cos * Q_SCALE
    sq = sin * Q_SCALE
    for c in range(ATTN_WIDTH // LANES):
        t = z[:, q0 + c * LANES:q0 + (c + 1) * LANES]
        q_ref[:, c * LANES:(c + 1) * LANES] = rope(t, cq, sq).astype(BF16)

    k0 = q0 + ATTN_WIDTH
    low = _lane_low_half((tm, LANES))
    kk = rope(z[:, k0:k0 + KV_WIDTH], cos, sin)
    vv = z[:, k0 + KV_WIDTH:k0 + 2 * KV_WIDTH]
    for src, dst in ((kk, k_ref), (vv, v_ref)):
        sw = pltpu.roll(src, HEAD_DIM, 1)
        dst[:, 0 * LANES:1 * LANES] = jnp.where(low, src, 0.0).astype(BF16)
        dst[:, 1 * LANES:2 * LANES] = jnp.where(low, 0.0, sw).astype(BF16)
        dst[:, 2 * LANES:3 * LANES] = jnp.where(low, sw, 0.0).astype(BF16)
        dst[:, 3 * LANES:4 * LANES] = jnp.where(low, 0.0, src).astype(BF16)

    m0 = k0 + 2 * KV_WIDTH
    zm_ref[...] = (z[:, m0:m0 + MEM_WIDTH] * Q_SCALE).astype(BF16)


def _in_proj_kernel(x_ref, *refs):
    _in_proj_body(x_ref[...], *refs)


def _in_proj_specs(l, b, s, tm):
    nt = s // tm
    tok = lambda w: pl.BlockSpec((None, tm, w), lambda i, j: (i, j, 0))
    seq_major = pl.BlockSpec((tm, FOURIER_WIDTH), lambda i, j: (j, i))
    rope_spec = pl.BlockSpec((tm, HEAD_DIM // 2), lambda i, j: (i * nt + j, 0))
    in_specs = [
        pl.BlockSpec((None, 1, D_MODEL), lambda i, j: (l, 0, 0)),
        pl.BlockSpec((D_MODEL, IN_WIDTH), lambda i, j: (0, 0), pipeline_mode=pl.Buffered(1)),
        pl.BlockSpec((None, FOURIER_WIDTH, 2 * FOURIER_WIDTH), lambda i, j: (l, 0, 0),
                     pipeline_mode=pl.Buffered(1)),
        rope_spec, rope_spec,
    ]
    out_specs = [tok(ATTN_WIDTH), tok(4 * LANES), tok(4 * LANES), tok(MEM_WIDTH), seq_major, seq_major]
    out_shape = [
        jax.ShapeDtypeStruct((b, s, ATTN_WIDTH), BF16),
        jax.ShapeDtypeStruct((b, s, 4 * LANES), BF16),
        jax.ShapeDtypeStruct((b, s, 4 * LANES), BF16),
        jax.ShapeDtypeStruct((b, s, MEM_WIDTH), BF16),
        jax.ShapeDtypeStruct((s, b * FOURIER_WIDTH), BF16),
        jax.ShapeDtypeStruct((s, b * FOURIER_WIDTH), BF16),
    ]
    return in_specs, out_specs, out_shape


def _in_proj(x, g_pre_mix, w_in_b, ab, cos_t, sin_t, *, l, tm):
    b, s, _ = x.shape
    in_specs, out_specs, out_shape = _in_proj_specs(l, b, s, tm)
    return pl.pallas_call(
        _in_proj_kernel,
        grid=(b, s // tm),
        in_specs=[pl.BlockSpec((None, tm, D_MODEL), lambda i, j: (i, j, 0))] + in_specs,
        out_specs=out_specs,
        out_shape=out_shape,
        compiler_params=pltpu.CompilerParams(
            dimension_semantics=("arbitrary", "arbitrary"), vmem_limit_bytes=VMEM_LIMIT),
        name="in_proj",
    )(x, g_pre_mix, w_in_b, ab, cos_t, sin_t)


def _cast_rows(n_rows, steps):
    rows = BF16_SUBLANES * pl.cdiv(n_rows, BF16_SUBLANES * steps)
    while n_rows % rows:
        rows += BF16_SUBLANES
    return rows


def _cast_specs(casts, n_inner, steps):
    in_specs, out_specs, out_shape = [], [], []
    for w, layer in casts:
        n_rows, n_cols = w.shape[1:]
        rows = _cast_rows(n_rows, steps)
        last = n_rows // rows - 1
        in_specs.append(pl.BlockSpec(
            (None, rows, n_cols),
            lambda i, j, layer=layer, last=last: (layer, jnp.minimum(i * n_inner + j, last), 0)))
        out_specs.append(pl.BlockSpec(
            (rows, n_cols), lambda i, j, last=last: (jnp.minimum(i * n_inner + j, last), 0)))
        out_shape.append(jax.ShapeDtypeStruct((n_rows, n_cols), BF16))
    return in_specs, out_specs, out_shape


def _cast_slices(srcs, dsts):
    for src, dst in zip(srcs, dsts):
        dst[...] = src[...].astype(BF16)


def _softmax_parts(sc, sink):
    m = jnp.max(sc, axis=-1, keepdims=True)
    if sink is None:
        return jnp.exp2(sc - m).astype(BF16), None
    m = jnp.maximum(m, sink)
    return jnp.exp2(sc - m).astype(BF16), jnp.exp2(sink - m)


def _head_indicator(rows):
    low = _lane_low_half((rows, LANES))
    top = jnp.where(low, 1.0, 0.0).astype(BF16)
    bot = jnp.where(low, 0.0, 1.0).astype(BF16)
    return jnp.concatenate([top, bot], axis=0)


def _attn_kernel(sink_ref, q_ref, k_ref, v_ref, zm_ref, kmc_ref, vmc_ref, gg_ref, *rest, seq, layer, n_cast):
    ya_ref, ym_ref = rest[n_cast:n_cast + 2]
    _cast_slices(rest[:n_cast], rest[n_cast + 2:])
    ga = gg_ref[:, FOURIER_WIDTH:FOURIER_WIDTH + ATTN_WIDTH]
    gm = gg_ref[:, FOURIER_WIDTH + ATTN_WIDTH:]
    tq = q_ref.shape[0]
    nblk = tq // BLOCK
    band = 3 * BLOCK
    t = pl.program_id(1)
    rel = (lax.broadcasted_iota(jnp.int32, (BLOCK, band), 0)
           - lax.broadcasted_iota(jnp.int32, (BLOCK, band), 1))
    low = _lane_low_half((BLOCK, LANES))
    pairs_per_kv = N_Q_HEADS // N_KV_HEADS // 2
    ind_band = _head_indicator(band)

    for jj in range(nblk):
        blk = t * nblk + jj
        s0 = pl.multiple_of(jnp.clip((blk - 1) * BLOCK, 0, seq - band), BLOCK)
        d = blk * BLOCK - s0
        valid = jnp.abs(rel + d) <= WINDOW
        rows = pl.ds(jj * BLOCK, BLOCK)
        outs = []
        for g in range(N_KV_HEADS):
            kcat = jnp.concatenate(
                [k_ref[pl.ds(s0, band), (2 * g) * LANES:(2 * g + 1) * LANES],
                 k_ref[pl.ds(s0, band), (2 * g + 1) * LANES:(2 * g + 2) * LANES]], axis=0)
            vcat = jnp.concatenate(
                [v_ref[pl.ds(s0, band), (2 * g) * LANES:(2 * g + 1) * LANES],
                 v_ref[pl.ds(s0, band), (2 * g + 1) * LANES:(2 * g + 2) * LANES]], axis=0)
            qg = jnp.concatenate(
                [q_ref[rows, (pairs_per_kv * g + p) * LANES:(pairs_per_kv * g + p + 1) * LANES]
                 for p in range(pairs_per_kv)], axis=0)
            sc = lax.dot_general(qg, kcat, (((1,), (1,)), ((), ())), preferred_element_type=F32)
            probs, sink_terms = [], []
            for p in range(pairs_per_kv):
                row_p, row_s = [], []
                for e in range(2):
                    head = (N_Q_HEADS // N_KV_HEADS) * g + 2 * p + e
                    sb = sc[p * BLOCK:(p + 1) * BLOCK, e * band:(e + 1) * band]
                    sb = jnp.where(valid, sb, NEG_INF)
                    pe, st = _softmax_parts(sb, sink_ref[layer, head] * LOG2E)
                    row_p.append(pe)
                    row_s.append(st)
                probs.append(jnp.concatenate(row_p, axis=1))
                sink_terms.append(row_s)
            pm = jnp.concatenate(probs, axis=0)
            o = jnp.dot(pm, jnp.concatenate([vcat, ind_band], axis=1), preferred_element_type=F32)
            for p in range(pairs_per_kv):
                op = o[p * BLOCK:(p + 1) * BLOCK, :]
                den = op[:, LANES:] + jnp.where(low, sink_terms[p][0], sink_terms[p][1])
                outs.append(op[:, :LANES] / den)
        y = jnp.concatenate(outs, axis=1)
        ya_ref[rows, :] = (_rms(y) * ga).astype(BF16)

    nmem = kmc_ref.shape[1] // 2
    ind_mem = _head_indicator(nmem)
    outs = []
    for a in range(N_MEM_HEADS // 2):
        qa = zm_ref[:, a * LANES:(a + 1) * LANES]
        sc = lax.dot_general(qa, kmc_ref[a], (((1,), (1,)), ((), ())), preferred_element_type=F32)
        p0, _ = _softmax_parts(sc[:, :nmem], None)
        p1, _ = _softmax_parts(sc[:, nmem:], None)
        o = jnp.dot(jnp.concatenate([p0, p1], axis=1), jnp.concatenate([vmc_ref[a], ind_mem], axis=1),
                    preferred_element_type=F32)
        outs.append(o[:, :LANES] / o[:, LANES:])
    ym = jnp.concatenate(outs, axis=1)
    ym_ref[...] = (_rms(ym) * gm).astype(BF16)


def _attention(sink, q, k4, v4, zm, kmc, vmc, g_grp, casts, *, l, tq):
    b, s, _ = q.shape
    nt = s // tq
    cin, cout, cshape = _cast_specs(casts, nt, b * nt)
    npair, m2, _ = kmc.shape[2:]
    tok = lambda w: pl.BlockSpec((None, tq, w), lambda i, j: (i, j, 0))
    per_batch = lambda w: pl.BlockSpec((None, s, w), lambda i, j: (i, 0, 0))
    memspec = pl.BlockSpec((None, None, npair, m2, LANES), lambda i, j: (l, i, 0, 0, 0))
    return pl.pallas_call(
        functools.partial(_attn_kernel, seq=s, layer=l, n_cast=len(casts)),
        grid=(b, nt),
        in_specs=[
            pl.BlockSpec(memory_space=pltpu.SMEM),
            tok(ATTN_WIDTH), per_batch(4 * LANES), per_batch(4 * LANES), tok(MEM_WIDTH),
            memspec, memspec,
            pl.BlockSpec((None, 1, D_MODEL), lambda i, j: (l, 0, 0)),
        ] + cin,
        out_specs=[tok(ATTN_WIDTH), tok(MEM_WIDTH)] + cout,
        out_shape=[jax.ShapeDtypeStruct((b, s, ATTN_WIDTH), BF16),
                   jax.ShapeDtypeStruct((b, s, MEM_WIDTH), BF16)] + cshape,
        compiler_params=pltpu.CompilerParams(
            dimension_semantics=("arbitrary", "arbitrary"), vmem_limit_bytes=VMEM_LIMIT),
        name="attention",
    )(sink, q, k4, v4, zm, kmc, vmc, g_grp, *(w for w, _ in casts))


def _exchange_matrix(size):
    ra = lax.broadcasted_iota(jnp.int32, (size, size), 0)
    rr = lax.broadcasted_iota(jnp.int32, (size, size), 1)
    return jnp.where(rr == size - ra, 1.0, 0.0).astype(BF16)


def _seq_dft_kernel(c_ref, s_ref, p_ref, q_ref, g_ref, y_ref, pp_ref, qm_ref, *, scale):
    n = p_ref.shape[0]
    half = n // 2

    @pl.when(pl.program_id(0) == 0)
    def _():
        flip = _exchange_matrix(PAIR_TILE)
        first = lax.broadcasted_iota(jnp.int32, (PAIR_TILE, p_ref.shape[1]), 0) == 0
        for a in range(half // PAIR_TILE):
            lo = a * PAIR_TILE
            src = n - lo - PAIR_TILE
            for x_ref, dst, sgn in ((p_ref, pp_ref, 1.0), (q_ref, qm_ref, -1.0)):
                rev = jnp.dot(flip, x_ref[src:src + PAIR_TILE, :], preferred_element_type=F32)
                if a > 0:
                    edge = x_ref[src + PAIR_TILE:src + PAIR_TILE + BF16_SUBLANES, :][0:1].astype(F32)
                    rev = jnp.where(first, edge, rev)
                dst[lo:lo + PAIR_TILE, :] = (x_ref[lo:lo + PAIR_TILE, :].astype(F32) + sgn * rev).astype(BF16)

    def group_norm(y):
        nb = y.shape[1] // FOURIER_WIDTH
        return jnp.concatenate(
            [(_rms(y[:, b * FOURIER_WIDTH:(b + 1) * FOURIER_WIDTH]) * g_ref[:, :FOURIER_WIDTH]).astype(BF16)
             for b in range(nb)], axis=1)

    mid = p_ref[half:half + BF16_SUBLANES, :][0:1].astype(F32)
    odd = (lax.broadcasted_iota(jnp.int32, (DFT_ROWS, 1), 0) & 1) == 1
    a1 = jnp.dot(c_ref[...], pp_ref[...], preferred_element_type=F32)
    a1 = (a1 + jnp.where(odd, -mid, mid)) * scale
    a2 = jnp.dot(s_ref[...], qm_ref[...], preferred_element_type=F32) * scale
    y_ref[0] = group_norm((a1 + a2)[:DFT_TILE])
    hi = group_norm(a1 - a2)
    mirrored = jnp.dot(_exchange_matrix(DFT_TILE), hi[:DFT_TILE], preferred_element_type=F32).astype(BF16)
    first = lax.broadcasted_iota(jnp.int32, mirrored.shape, 0) == 0
    y_ref[1] = jnp.where(first, hi[DFT_TILE:DFT_TILE + 1], mirrored)


def _seq_dft(cmat, smat, p2, q2, g_grp, *, l):
    s, w = p2.shape
    nt = cmat.shape[0]
    scale = 1.0 / math.sqrt(s * FOURIER_CH)
    resident = pl.BlockSpec((s, w), lambda i: (0, 0), pipeline_mode=pl.Buffered(1))
    half = pl.BlockSpec((None, DFT_ROWS, s // 2), lambda i: (i, 0, 0))
    return pl.pallas_call(
        functools.partial(_seq_dft_kernel, scale=scale),
        grid=(nt,),
        in_specs=[half, half, resident, resident, pl.BlockSpec((None, 1, D_MODEL), lambda i: (l, 0, 0))],
        out_specs=pl.BlockSpec((2, DFT_TILE, w), lambda i: (0, i, 0)),
        out_shape=jax.ShapeDtypeStruct((2, s // 2, w), BF16),
        scratch_shapes=[pltpu.VMEM((s // 2, w), BF16), pltpu.VMEM((s // 2, w), BF16)],
        compiler_params=pltpu.CompilerParams(
            dimension_semantics=("arbitrary",), vmem_limit_bytes=VMEM_LIMIT),
        name="seq_dft",
    )(cmat, smat, p2, q2, g_grp)


def _out_ffn_kernel(yf_ref, ya_ref, ym_ref, x_ref, wo_ref, gpm_ref, gpf_ref, w1_ref, w2_ref, gpo_ref,
                    *rest, fuse_next, n_cast):
    n_next_in = 5 if fuse_next else 0
    n_next_out = 6 if fuse_next else 0
    next_in = rest[:n_next_in]
    cast_src = rest[n_next_in:n_next_in + n_cast]
    outs = rest[n_next_in + n_cast:-3]
    xo_ref, next_out, cast_dst = outs[0], outs[1:1 + n_next_out], outs[1 + n_next_out:]
    x1_ref, h_ref, acc_ref = rest[-3:]
    _cast_slices(cast_src, cast_dst)
    ycat = jnp.concatenate([yf_ref[...], ya_ref[...], ym_ref[...]], axis=1)
    y = jnp.dot(ycat, wo_ref[...], preferred_element_type=F32)
    x1 = x_ref[...] + _rms(y) * gpm_ref[...]
    x1_ref[...] = x1
    h_ref[...] = (_rms(x1) * gpf_ref[...]).astype(BF16)
    for c in range(N_FF_CHUNKS):
        cols = slice(c * FF_CHUNK, (c + 1) * FF_CHUNK)
        gate = jnp.dot(h_ref[...], w1_ref[:, cols], preferred_element_type=F32)
        up = jnp.dot(h_ref[...], w1_ref[:, D_FF + c * FF_CHUNK:D_FF + (c + 1) * FF_CHUNK],
                     preferred_element_type=F32)
        f = (gate * jax.nn.sigmoid(gate) * up).astype(BF16)
        part = jnp.dot(f, w2_ref[cols, :], preferred_element_type=F32)
        if c == 0:
            acc_ref[...] = part
        else:
            acc_ref[...] += part
    x2 = x1_ref[...] + _rms(acc_ref[...]) * gpo_ref[...]
    xo_ref[...] = x2
    if fuse_next:
        _in_proj_body(x2, *next_in, *next_out)


def _out_ffn(yf, ya, ym, x, w_out_l, g_post_mix, g_pre_ffn, w1_l, w2_l, g_post_ffn, next_args, casts, *, l, tm):
    b, s, _ = x.shape
    nt = s // tm
    steps = b * nt
    assert tm == DFT_TILE and nt % 2 == 0
    tok = lambda w: pl.BlockSpec((None, tm, w), lambda i, j: (i, j, 0))
    vec = pl.BlockSpec((None, 1, D_MODEL), lambda i, j: (l, 0, 0))
    one = pl.Buffered(1)
    flat = lambda r, c: pl.BlockSpec((r, c), lambda i, j: (0, 0), pipeline_mode=one)
    in_specs = [
        pl.BlockSpec((None, tm, FOURIER_WIDTH),
                     lambda i, j: (j // (nt // 2), jnp.where(j < nt // 2, j, nt - 1 - j), i)),
        tok(ATTN_WIDTH), tok(MEM_WIDTH), tok(D_MODEL),
        flat(D_MODEL, D_MODEL), vec, vec, flat(D_MODEL, 2 * D_FF), flat(D_FF, D_MODEL), vec,
    ]
    out_specs = [tok(D_MODEL)]
    out_shape = [jax.ShapeDtypeStruct((b, s, D_MODEL), F32)]
    fuse_next = next_args is not None
    if fuse_next:
        nin, nout, nshape = _in_proj_specs(l + 1, b, s, tm)
        in_specs += nin
        out_specs += nout
        out_shape += nshape
    cin, cout, cshape = _cast_specs(casts, nt, steps)
    in_specs += cin
    out_specs += cout
    out_shape += cshape
    return pl.pallas_call(
        functools.partial(_out_ffn_kernel, fuse_next=fuse_next, n_cast=len(casts)),
        grid=(b, nt),
        in_specs=in_specs,
        out_specs=out_specs,
        out_shape=out_shape,
        scratch_shapes=[pltpu.VMEM((tm, D_MODEL), F32), pltpu.VMEM((tm, D_MODEL), BF16),
                        pltpu.VMEM((tm, D_MODEL), F32)],
        compiler_params=pltpu.CompilerParams(
            dimension_semantics=("arbitrary", "arbitrary"), vmem_limit_bytes=VMEM_LIMIT),
        name="out_ffn",
    )(yf, ya, ym, x, w_out_l, g_post_mix, g_pre_ffn, w1_l, w2_l, g_post_ffn,
      *(next_args or ()), *(w for w, _ in casts))


def kernel(x, mem, positions, g_pre_mix, w_in, w_fourier, sink, g_mem, w_mem_kv, g_grp,
           w_out, g_post_mix, g_pre_ffn, w_ffn_in, w_ffn_out, g_post_ffn):
    b, s, _ = x.shape
    assert s % ATTN_TILE == 0 and s >= 3 * BLOCK

    w_in_b = {0: w_in[0].astype(BF16)}
    w_out_b, w1, w2 = {}, {}, {}

    cos_t, sin_t = _rope_tables(positions)
    cmat, smat = _dft_matrices(s)
    ab = _fourier_fold(w_fourier)
    kmc, vmc = _mem_kv(mem, g_mem, w_mem_kv)

    vec3 = lambda g: g.reshape(DEPTH, 1, g.shape[-1])
    g_pre_mix, g_grp, g_post_mix, g_pre_ffn, g_post_ffn = map(
        vec3, (g_pre_mix, g_grp, g_post_mix, g_pre_ffn, g_post_ffn))

    in_args = lambda l: (g_pre_mix, w_in_b[l], ab, cos_t, sin_t)
    mix = _in_proj(x, *in_args(0), l=0, tm=2 * TOKEN_TILE)
    for l in range(DEPTH):
        q, k4, v4, zm, p2, q2 = mix
        first = [(w_out, 0), (w_ffn_in, 0), (w_ffn_out, 0), (w_in, 1)] if l == 0 else []
        att = _attention(sink, q, k4, v4, zm, kmc, vmc, g_grp, first, l=l, tq=ATTN_TILE)
        ya, ym = att[:2]
        if l == 0:
            w_out_b[0], w1[0], w2[0], w_in_b[1] = att[2:]
        yf = _seq_dft(cmat, smat, p2, q2, g_grp, l=l)
        casts = []
        if l + 1 < DEPTH:
            casts += [(w_out, l + 1), (w_ffn_in, l + 1), (w_ffn_out, l + 1)]
        if l + 2 < DEPTH:
            casts += [(w_in, l + 2)]
        res = _out_ffn(yf, ya, ym, x, w_out_b[l], g_post_mix, g_pre_ffn, w1[l], w2[l], g_post_ffn,
                       in_args(l + 1) if l + 1 < DEPTH else None, casts, l=l, tm=TOKEN_TILE)
        x = res[0]
        if l + 1 < DEPTH:
            mix = res[1:7]
            w_out_b[l + 1], w1[l + 1], w2[l + 1] = res[7:10]
        if l + 2 < DEPTH:
            w_in_b[l + 2] = res[10]
    return x
```
